```python
import jax, jax.numpy as jnp
from jax import lax
import numpy as np

D_MODEL = 1024
BATCH = 2
SEQ = 16384
DEPTH = 2

D_FF = 2816
N_SUB = 3
W_A = D_MODEL // 2
W_B = D_MODEL // 2
H_A = 8
DH_A = W_A // H_A
G_B = 8
DG_B = W_B // G_B
CONV_A = 4
CONV_B = 31
LRU_C = 8.0
W_C = D_MODEL
H_C = 8
DH_C = W_C // H_C
CHUNK = 128
EPS = 1e-6

kernel_name = "hybrid_rglru_conformer_gmlp_block"


def _rmsnorm(x, g):
    x32 = x.astype(jnp.float32)
    y = x32 * lax.rsqrt(jnp.mean(x32 * x32, axis=-1, keepdims=True) + EPS)
    return (y * g.astype(jnp.float32)).astype(x.dtype)


def _layernorm(x, g, b):
    x32 = x.astype(jnp.float32)
    mu = jnp.mean(x32, axis=-1, keepdims=True)
    var = jnp.mean(jnp.square(x32 - mu), axis=-1, keepdims=True)
    y = (x32 - mu) * lax.rsqrt(var + EPS)
    return (y * g.astype(jnp.float32) + b.astype(jnp.float32)).astype(x.dtype)


def _causal_dwconv(x, w, b):
    k = w.shape[0]
    y = lax.conv_general_dilated(
        x, w[:, None, :].astype(x.dtype), window_strides=(1,), padding=[(k - 1, 0)],
        dimension_numbers=("NWC", "WIO", "NWC"), feature_group_count=x.shape[-1])
    return y + b


def _swiglu(h, w13, w2):
    g, u = jnp.split(h @ w13, 2, axis=-1)
    return (jax.nn.silu(g) * u) @ w2


def _rg_lru(xr, gate_w, gate_b, lam):
    bsz, s, _ = xr.shape
    xh = xr.reshape(bsz, s, H_A, DH_A)
    gates = (jnp.einsum("bshd,hde->bshe", xh, gate_w) + gate_b).astype(jnp.float32)
    r, i = jnp.split(jax.nn.sigmoid(gates), 2, axis=-1)
    r = r.reshape(bsz, s, W_A)
    i = i.reshape(bsz, s, W_A)
    log_a = LRU_C * r * jax.nn.log_sigmoid(lam.astype(jnp.float32))
    a = jnp.exp(log_a)
    u = jnp.sqrt(-jnp.expm1(2.0 * log_a)) * (i * xr.astype(jnp.float32))

    def combine(left, right):
        a1, b1 = left
        a2, b2 = right
        return a1 * a2, a2 * b1 + b2

    _, h = lax.associative_scan(combine, (a, u), axis=1)
    return h.astype(xr.dtype)


def _mixer_ab(h, w_in, a_conv_w, a_conv_b, a_gate_w, a_gate_b, a_lam,
              b_conv_w, b_conv_b, b_norm_g, b_norm_b, w_out):
    z = h @ w_in
    a_gate, a_x, b_val, b_gate = jnp.split(z, [W_A, 2 * W_A, 2 * W_A + W_B], axis=-1)
    a_x = _causal_dwconv(a_x, a_conv_w, a_conv_b)
    y_a = _rg_lru(a_x, a_gate_w, a_gate_b, a_lam) * jax.nn.gelu(a_gate)
    v = b_val * jax.nn.sigmoid(b_gate)
    v = _causal_dwconv(v, b_conv_w, b_conv_b)
    bsz, s, _ = v.shape
    v = _layernorm(v.reshape(bsz, s, G_B, DG_B), b_norm_g.reshape(G_B, DG_B),
                   b_norm_b.reshape(G_B, DG_B)).reshape(bsz, s, W_B)
    y_b = jax.nn.silu(v)
    return jnp.concatenate([y_a, y_b], axis=-1) @ w_out


def _mixer_c(h, w_in, b_in, norm_g, norm_b, w_s, b_s, w_out):
    z = jax.nn.gelu(h @ w_in + b_in)
    u, v = jnp.split(z, 2, axis=-1)
    v = _layernorm(v, norm_g, norm_b)
    bsz, s, _ = v.shape
    vc = v.reshape(bsz, s // CHUNK, CHUNK, H_C, DH_C)
    mask = jnp.tril(jnp.ones((CHUNK, CHUNK), dtype=bool))
    ws = jnp.where(mask, w_s, jnp.zeros_like(w_s)).astype(v.dtype)
    mixed = jnp.einsum("hts,bnshd->bnthd", ws, vc) + jnp.transpose(b_s)[:, :, None]
    return (u * mixed.reshape(bsz, s, W_C)) @ w_out


def _sublayer(x, fn, pre_g, post_g, shift, scale, gate, res_w):
    h = _rmsnorm(x, pre_g) * (1.0 + scale[:, None, :]) + shift[:, None, :]
    y = _rmsnorm(fn(h), post_g)
    return x + res_w * (1.0 + gate[:, None, :]) * y


def setup_inputs(seed: int = 0) -> dict:
    key = jax.random.key(seed)
    ks = jax.random.split(key, 32)
    ne = (DEPTH + 1) // 2
    no = DEPTH // 2
    f32 = jnp.float32

    def nrm(k, shape, scale):
        return jax.random.normal(k, shape, f32) * scale

    u_lam = jax.random.uniform(ks[14], (ne, W_A), f32, minval=0.9, maxval=0.999)
    sa = u_lam ** (1.0 / LRU_C)
    a_lam = jnp.log(sa) - jnp.log1p(-sa)
    return {
        "x": nrm(ks[0], (BATCH, SEQ, D_MODEL), 1.0),
        "c": nrm(ks[1], (BATCH, D_MODEL), 1.0),
        "ada_w": nrm(ks[2], (DEPTH, D_MODEL, N_SUB * 3 * D_MODEL), 0.1 * D_MODEL ** -0.5),
        "ada_b": nrm(ks[3], (DEPTH, N_SUB * 3 * D_MODEL), 0.01),
        "norm_pre": 1.0 + nrm(ks[4], (DEPTH, N_SUB, D_MODEL), 0.02),
        "norm_post": 1.0 + nrm(ks[5], (DEPTH, N_SUB, D_MODEL), 0.02),
        "ffn_w13": nrm(ks[6], (DEPTH, 2, D_MODEL, 2 * D_FF), D_MODEL ** -0.5),
        "ffn_w2": nrm(ks[7], (DEPTH, 2, D_FF, D_MODEL), D_FF ** -0.5),
        "ab_w_in": nrm(ks[8], (ne, D_MODEL, 2 * W_A + 2 * W_B), D_MODEL ** -0.5),
        "a_conv_w": nrm(ks[9], (ne, CONV_A, W_A), CONV_A ** -0.5),
        "a_conv_b": nrm(ks[10], (ne, W_A), 0.01),
        "a_gate_w": nrm(ks[11], (ne, H_A, DH_A, 2 * DH_A), DH_A ** -0.5),
        "a_gate_b": nrm(ks[12], (ne, H_A, 2 * DH_A), 0.01),
        "a_lam": a_lam,
        "b_conv_w": nrm(ks[15], (ne, CONV_B, W_B), CONV_B ** -0.5),
        "b_conv_b": nrm(ks[16], (ne, W_B), 0.01),
        "b_norm_g": 1.0 + nrm(ks[17], (ne, W_B), 0.02),
        "b_norm_b": nrm(ks[18], (ne, W_B), 0.01),
        "ab_w_out": nrm(ks[19], (ne, W_A + W_B, D_MODEL), (W_A + W_B) ** -0.5),
        "c_w_in": nrm(ks[20], (no, D_MODEL, 2 * W_C), D_MODEL ** -0.5),
        "c_b_in": nrm(ks[21], (no, 2 * W_C), 0.01),
        "c_norm_g": 1.0 + nrm(ks[22], (no, W_C), 0.02),
        "c_norm_b": nrm(ks[23], (no, W_C), 0.01),
        "c_w_s": nrm(ks[24], (no, H_C, CHUNK, CHUNK), 0.5 * CHUNK ** -0.5),
        "c_b_s": 1.0 + nrm(ks[25], (no, H_C, CHUNK), 0.01),
        "c_w_out": nrm(ks[26], (no, W_C, D_MODEL), W_C ** -0.5),
    }


def reference(x, c, ada_w, ada_b, norm_pre, norm_post, ffn_w13, ffn_w2,
              ab_w_in, a_conv_w, a_conv_b, a_gate_w, a_gate_b, a_lam,
              b_conv_w, b_conv_b, b_norm_g, b_norm_b, ab_w_out,
              c_w_in, c_b_in, c_norm_g, c_norm_b, c_w_s, c_b_s, c_w_out):
    bsz = x.shape[0]
    c_act = jax.nn.silu(c)
    for l in range(DEPTH):
        mod = (c_act @ ada_w[l] + ada_b[l]).reshape(bsz, N_SUB, 3, D_MODEL)

        def ffn_pre(h, l=l):
            return _swiglu(h, ffn_w13[l, 0], ffn_w2[l, 0])

        def ffn_post(h, l=l):
            return _swiglu(h, ffn_w13[l, 1], ffn_w2[l, 1])

        if l % 2 == 0:
            k = l // 2

            def mixer(h, k=k):
                return _mixer_ab(h, ab_w_in[k], a_conv_w[k], a_conv_b[k], a_gate_w[k],
                                 a_gate_b[k], a_lam[k], b_conv_w[k], b_conv_b[k],
                                 b_norm_g[k], b_norm_b[k], ab_w_out[k])
        else:
            k = l // 2

            def mixer(h, k=k):
                return _mixer_c(h, c_w_in[k], c_b_in[k], c_norm_g[k], c_norm_b[k],
                                c_w_s[k], c_b_s[k], c_w_out[k])

        x = _sublayer(x, ffn_pre, norm_pre[l, 0], norm_post[l, 0],
                      mod[:, 0, 0], mod[:, 0, 1], mod[:, 0, 2], 0.5)
        x = _sublayer(x, mixer, norm_pre[l, 1], norm_post[l, 1],
                      mod[:, 1, 0], mod[:, 1, 1], mod[:, 1, 2], 1.0)
        x = _sublayer(x, ffn_post, norm_pre[l, 2], norm_post[l, 2],
                      mod[:, 2, 0], mod[:, 2, 1], mod[:, 2, 2], 0.5)
    return x
```

```python
import functools

import jax
import jax.numpy as jnp
from jax import lax
from jax.experimental import pallas as pl
from jax.experimental.pallas import tpu as pltpu

D_MODEL = 1024
D_FF = 2816
N_SUB = 3
W_A = D_MODEL // 2
W_B = D_MODEL // 2
H_A = 8
DH_A = W_A // H_A
G_B = 8
DG_B = W_B // G_B
CONV_A = 4
CONV_B = 31
LRU_C = 8.0
W_C = D_MODEL
H_C = 8
DH_C = W_C // H_C
CHUNK = 128
EPS = 1e-6

F32 = jnp.float32
BF16 = jnp.bfloat16

VMEM_LIMIT_BYTES = 56 * 1024 * 1024
SUBLANES = 8
FFN_ROWS = 512
FFN_CHUNK = 256
MIX_ROWS = 512
ADA_COLS = 1536
ADA_ROWS = 16
HALO_A = 8
HALO_B = 32
CONV_ROWS = 32


def _const_spec(shape):
    nd = len(shape)
    return pl.BlockSpec(shape, lambda *_: (0,) * nd, pipeline_mode=pl.Buffered(1))


def _prenorm(x, mod, pre_g):
    ms = jnp.mean(x * x, axis=-1, keepdims=True)
    y = x * lax.rsqrt(ms + EPS) * pre_g
    return y * (1.0 + mod[1:2]) + mod[0:1]


def _postnorm_residual(x, y, mod, post_g, res_w):
    ms = jnp.mean(y * y, axis=-1, keepdims=True)
    yn = y * lax.rsqrt(ms + EPS) * post_g
    return x + (res_w * (1.0 + mod[2:3])) * yn


def _dot(a, b):
    return jnp.dot(a, b, preferred_element_type=F32)


def _ada_kernel(c_ref, w_ref, b_ref, o_ref):
    ca = jax.nn.silu(c_ref[...]).astype(BF16)
    o_ref[0] = _dot(ca, w_ref[0].astype(BF16)) + b_ref[0]


def _ada_call(c_pad, ada_w, ada_b):
    depth, d, n_out = ada_w.shape
    return pl.pallas_call(
        _ada_kernel,
        grid=(depth, n_out // ADA_COLS),
        in_specs=[
            pl.BlockSpec((ADA_ROWS, d), lambda l, j: (0, 0)),
            pl.BlockSpec((1, d, ADA_COLS), lambda l, j: (l, 0, j)),
            pl.BlockSpec((1, 1, ADA_COLS), lambda l, j: (l, 0, j)),
        ],
        out_specs=pl.BlockSpec((1, ADA_ROWS, ADA_COLS), lambda l, j: (l, 0, j)),
        out_shape=jax.ShapeDtypeStruct((depth, ADA_ROWS, n_out), F32),
        compiler_params=pltpu.CompilerParams(
            dimension_semantics=("arbitrary", "arbitrary"),
            vmem_limit_bytes=VMEM_LIMIT_BYTES),
        name="ada_proj",
    )(c_pad, ada_w, ada_b.reshape(depth, 1, n_out))


def _ffn_kernel(x_ref, mod_ref, pre_ref, post_ref, w13_ref, w2_ref, o_ref, h_ref, *, res_w):
    mod = mod_ref[0]
    h_ref[...] = _prenorm(x_ref[...], mod, pre_ref[...]).astype(BF16)
    acc = None
    for c in range(D_FF // FFN_CHUNK):
        lo = c * FFN_CHUNK
        h = h_ref[...]
        g = _dot(h, w13_ref[:, lo:lo + FFN_CHUNK])
        u = _dot(h, w13_ref[:, D_FF + lo:D_FF + lo + FFN_CHUNK])
        act = (jax.nn.silu(g) * u).astype(BF16)
        p = _dot(act, w2_ref[lo:lo + FFN_CHUNK, :])
        acc = p if acc is None else acc + p
    o_ref[...] = _postnorm_residual(x_ref[...], acc, mod, post_ref[...], res_w)


def _ffn_call(x2, mod, pre_g, post_g, w13, w2, *, res_w, rows_per_batch):
    n, d = x2.shape
    tiles_per_batch = rows_per_batch // FFN_ROWS
    return pl.pallas_call(
        functools.partial(_ffn_kernel, res_w=res_w),
        grid=(n // FFN_ROWS,),
        in_specs=[
            pl.BlockSpec((FFN_ROWS, d), lambda i: (i, 0)),
            pl.BlockSpec((1, 3, d), lambda i: (i // tiles_per_batch, 0, 0)),
            _const_spec((1, d)),
            _const_spec((1, d)),
            _const_spec(w13.shape),
            _const_spec(w2.shape),
        ],
        out_specs=pl.BlockSpec((FFN_ROWS, d), lambda i: (i, 0)),
        out_shape=jax.ShapeDtypeStruct((n, d), F32),
        scratch_shapes=[pltpu.VMEM((FFN_ROWS, d), BF16)],
        compiler_params=pltpu.CompilerParams(
            dimension_semantics=("arbitrary",),
            vmem_limit_bytes=VMEM_LIMIT_BYTES),
        name="ffn_sublayer",
    )(x2, mod, pre_g, post_g, w13, w2)


def _causal_conv(buf_ref, w_ref, b_ref, out_ref, *, taps, halo, rows):
    base = halo - (taps - 1)
    bias = b_ref[...]
    for r in range(rows // CONV_ROWS):
        r0 = r * CONV_ROWS
        acc = jnp.broadcast_to(bias, (CONV_ROWS, bias.shape[-1]))
        for k in range(taps):
            acc = acc + w_ref[k:k + 1, :] * buf_ref[r0 + base + k:r0 + base + k + CONV_ROWS, :]
        out_ref[r0:r0 + CONV_ROWS, :] = acc


def _ab_kernel(x_ref, mod_ref, pre_ref, post_ref, win_ref, acw_ref, acb_ref, gw_ref, gb_ref,
               lam_ref, bcw_ref, bcb_ref, ng_ref, nb_ref, avg_ref, wout_ref, o_ref,
               ax_buf, v_buf, a_buf, u_buf, t_buf, hc_ref):
    rows = x_ref.shape[0]
    j = pl.program_id(1)

    @pl.when(j == 0)
    def _():
        ax_buf[0:HALO_A, :] = jnp.zeros((HALO_A, W_A), F32)
        v_buf[0:HALO_B, :] = jnp.zeros((HALO_B, W_B), F32)
        hc_ref[...] = jnp.zeros_like(hc_ref)

    mod = mod_ref[0]
    h = _prenorm(x_ref[...], mod, pre_ref[...]).astype(BF16)
    z = _dot(h, win_ref[...])
    a_gate = z[:, 0:W_A]
    b_val = z[:, 2 * W_A:2 * W_A + W_B]
    b_gate = z[:, 2 * W_A + W_B:]

    ax_buf[HALO_A:HALO_A + rows, :] = z[:, W_A:2 * W_A]
    _causal_conv(ax_buf, acw_ref, acb_ref, t_buf, taps=CONV_A, halo=HALO_A, rows=rows)
    ax_buf[0:HALO_A, :] = ax_buf[rows:rows + HALO_A, :]
    xr = t_buf[...]
    gates = _dot(xr.astype(BF16), gw_ref[...]) + gb_ref[...]
    r_gate = jax.nn.sigmoid(gates[:, 0:W_A])
    i_gate = jax.nn.sigmoid(gates[:, W_A:])
    log_a = (LRU_C * r_gate) * jax.nn.log_sigmoid(lam_ref[...])
    a_buf[...] = jnp.exp(log_a)
    th = jnp.tanh(log_a)
    u_buf[...] = jnp.sqrt((-2.0 * th) / (1.0 - th)) * (i_gate * xr)

    row_id = lax.broadcasted_iota(jnp.int32, (SUBLANES, W_A), 0)

    def scan_step(i, h_prev):
        r0 = pl.multiple_of(i * SUBLANES, SUBLANES)
        a = a_buf[pl.ds(r0, SUBLANES), :]
        hh = u_buf[pl.ds(r0, SUBLANES), :]
        for k in (1, 2, 4):
            keep = row_id >= k
            h_sh = jnp.where(keep, pltpu.roll(hh, k, 0), 0.0)
            a_sh = jnp.where(keep, pltpu.roll(a, k, 0), 1.0)
            hh = hh + a * h_sh
            a = a * a_sh
        hh = hh + a * h_prev
        u_buf[pl.ds(r0, SUBLANES), :] = hh
        return jnp.broadcast_to(hh[SUBLANES - 1:SUBLANES, :], (SUBLANES, W_A))

    hc_ref[...] = lax.fori_loop(0, rows // SUBLANES, scan_step, hc_ref[...])
    y_a = u_buf[...] * jax.nn.gelu(a_gate)

    v_buf[HALO_B:HALO_B + rows, :] = b_val * jax.nn.sigmoid(b_gate)
    _causal_conv(v_buf, bcw_ref, bcb_ref, t_buf, taps=CONV_B, halo=HALO_B, rows=rows)
    v_buf[0:HALO_B, :] = v_buf[rows:rows + HALO_B, :]
    v = t_buf[...]
    avg = avg_ref[...]

    def group_mean(t):
        hi = t.astype(BF16)
        lo = (t - hi.astype(F32)).astype(BF16)
        return _dot(hi, avg) + _dot(lo, avg)

    cen = v - group_mean(v)
    var = group_mean(cen * cen)
    vn = cen * lax.rsqrt(var + EPS) * ng_ref[...] + nb_ref[...]
    y_b = jax.nn.silu(vn)

    y = _dot(y_a.astype(BF16), wout_ref[0:W_A, :]) + _dot(y_b.astype(BF16), wout_ref[W_A:, :])
    o_ref[...] = _postnorm_residual(x_ref[...], y, mod, post_ref[...], 1.0)


def _ab_call(x2, mod, pre_g, post_g, w_in, a_conv_w, a_conv_b, gate_w, gate_b, lam,
             b_conv_w, b_conv_b, norm_g, norm_b, avg, w_out, *, batch, rows_per_batch):
    n, d = x2.shape
    rows = MIX_ROWS
    tiles = rows_per_batch // rows
    consts = (pre_g, post_g, w_in, a_conv_w, a_conv_b, gate_w, gate_b, lam,
              b_conv_w, b_conv_b, norm_g, norm_b, avg, w_out)
    return pl.pallas_call(
        _ab_kernel,
        grid=(batch, tiles),
        in_specs=[
            pl.BlockSpec((rows, d), lambda b, j: (b * tiles + j, 0)),
            pl.BlockSpec((1, 3, d), lambda b, j: (b, 0, 0)),
        ] + [_const_spec(a.shape) for a in consts],
        out_specs=pl.BlockSpec((rows, d), lambda b, j: (b * tiles + j, 0)),
        out_shape=jax.ShapeDtypeStruct((n, d), F32),
        scratch_shapes=[
            pltpu.VMEM((HALO_A + rows, W_A), F32),
            pltpu.VMEM((HALO_B + rows, W_B), F32),
            pltpu.VMEM((rows, W_A), F32),
            pltpu.VMEM((rows, W_A), F32),
            pltpu.VMEM((rows, W_A), F32),
            pltpu.VMEM((SUBLANES, W_A), F32),
        ],
        compiler_params=pltpu.CompilerParams(
            dimension_semantics=("arbitrary", "arbitrary"),
            vmem_limit_bytes=VMEM_LIMIT_BYTES),
        name="mixer_ab_sublayer",
    )(x2, mod, *consts)


def _c_kernel(x_ref, mod_ref, pre_ref, post_ref, win_ref, bin_ref, ng_ref, nb_ref,
              ws_ref, bs_ref, wout_ref, o_ref):
    rows = x_ref.shape[0]
    n_chunks = rows // CHUNK
    mod = mod_ref[0]
    h = _prenorm(x_ref[...], mod, pre_ref[...]).astype(BF16)
    z = jax.nn.gelu(_dot(h, win_ref[...]) + bin_ref[...])
    u = z[:, 0:W_C]
    v = z[:, W_C:]
    mu = jnp.mean(v, axis=-1, keepdims=True)
    cen = v - mu
    var = jnp.mean(cen * cen, axis=-1, keepdims=True)
    vb = (cen * lax.rsqrt(var + EPS) * ng_ref[...] + nb_ref[...]).astype(BF16)

    t_id = lax.broadcasted_iota(jnp.int32, (CHUNK, CHUNK), 0)
    s_id = lax.broadcasted_iota(jnp.int32, (CHUNK, CHUNK), 1)
    causal = s_id <= t_id
    per_head = []
    for hd in range(H_C):
        ws = jnp.where(causal, ws_ref[hd], 0.0).astype(BF16)
        c0 = hd * DH_C
        vh = jnp.concatenate(
            [vb[n * CHUNK:(n + 1) * CHUNK, c0:c0 + DH_C] for n in range(n_chunks)], axis=1)
        per_head.append(_dot(ws, vh))
    mixed = jnp.concatenate(
        [jnp.concatenate([per_head[hd][:, n * DH_C:(n + 1) * DH_C] for hd in range(H_C)], axis=1)
         + bs_ref[...] for n in range(n_chunks)], axis=0)
    y = _dot((u * mixed).astype(BF16), wout_ref[...])
    o_ref[...] = _postnorm_residual(x_ref[...], y, mod, post_ref[...], 1.0)


def _c_call(x2, mod, pre_g, post_g, w_in, b_in, norm_g, norm_b, w_s, bias_s, w_out,
            *, rows_per_batch):
    n, d = x2.shape
    rows = MIX_ROWS
    tiles_per_batch = rows_per_batch // rows
    consts = (pre_g, post_g, w_in, b_in, norm_g, norm_b, w_s, bias_s, w_out)
    return pl.pallas_call(
        _c_kernel,
        grid=(n // rows,),
        in_specs=[
            pl.BlockSpec((rows, d), lambda i: (i, 0)),
            pl.BlockSpec((1, 3, d), lambda i: (i // tiles_per_batch, 0, 0)),
        ] + [_const_spec(a.shape) for a in consts],
        out_specs=pl.BlockSpec((rows, d), lambda i: (i, 0)),
        out_shape=jax.ShapeDtypeStruct((n, d), F32),
        compiler_params=pltpu.CompilerParams(
            dimension_semantics=("arbitrary",),
            vmem_limit_bytes=VMEM_LIMIT_BYTES),
        name="mixer_c_sublayer",
    )(x2, mod, *consts)


def _row(v):
    return v.reshape(1, -1)


def _block_diag_gate(gate_w, gate_b):
    eye = jnp.eye(H_A, dtype=gate_w.dtype)
    w_r = jnp.einsum("hde,hg->hdge", gate_w[:, :, :DH_A], eye).reshape(W_A, W_A)
    w_i = jnp.einsum("hde,hg->hdge", gate_w[:, :, DH_A:], eye).reshape(W_A, W_A)
    w = jnp.concatenate([w_r, w_i], axis=1).astype(BF16)
    b = jnp.concatenate([gate_b[:, :DH_A].reshape(-1), gate_b[:, DH_A:].reshape(-1)])
    return w, _row(b)


def kernel(x, c, ada_w, ada_b, norm_pre, norm_post, ffn_w13, ffn_w2, ab_w_in, a_conv_w, a_conv_b, a_gate_w, a_gate_b, a_lam, b_conv_w, b_conv_b, b_norm_g, b_norm_b, ab_w_out, c_w_in, c_b_in, c_norm_g, c_norm_b, c_w_s, c_b_s, c_w_out):
    bsz, seq, d = x.shape
    depth = ada_w.shape[0]
    assert d == D_MODEL and seq % FFN_ROWS == 0 and seq % MIX_ROWS == 0 and bsz <= ADA_ROWS
    x2 = x.reshape(bsz * seq, d)

    c_pad = jnp.zeros((ADA_ROWS, d), F32).at[:bsz].set(c)
    mods = _ada_call(c_pad, ada_w, ada_b)[:, :bsz].reshape(depth, bsz, N_SUB, 3, d)

    group_id = jnp.arange(W_B) // DG_B
    avg = jnp.where(group_id[:, None] == group_id[None, :], 1.0 / DG_B, 0.0).astype(BF16)

    for l in range(depth):
        k = l // 2

        def ffn(x2, s, i, l=l):
            return _ffn_call(
                x2, mods[l, :, s], _row(norm_pre[l, s]), _row(norm_post[l, s]),
                ffn_w13[l, i].astype(BF16), ffn_w2[l, i].astype(BF16),
                res_w=0.5, rows_per_batch=seq)

        x2 = ffn(x2, 0, 0)
        if l % 2 == 0:
            gate_w, gate_b = _block_diag_gate(a_gate_w[k], a_gate_b[k])
            x2 = _ab_call(
                x2, mods[l, :, 1], _row(norm_pre[l, 1]), _row(norm_post[l, 1]),
                ab_w_in[k].astype(BF16), a_conv_w[k], _row(a_conv_b[k]), gate_w, gate_b,
                _row(a_lam[k]), b_conv_w[k], _row(b_conv_b[k]), _row(b_norm_g[k]),
                _row(b_norm_b[k]), avg, ab_w_out[k].astype(BF16),
                batch=bsz, rows_per_batch=seq)
        else:
            bias_s = jnp.repeat(jnp.transpose(c_b_s[k]), DH_C, axis=1)
            x2 = _c_call(
                x2, mods[l, :, 1], _row(norm_pre[l, 1]), _row(norm_post[l, 1]),
                c_w_in[k].astype(BF16), _row(c_b_in[k]), _row(c_norm_g[k]), _row(c_norm_b[k]),
                c_w_s[k], bias_s, c_w_out[k].astype(BF16), rows_per_batch=seq)
        x2 = ffn(x2, 2, 1)
    return x2.reshape(bsz, seq, d)
```

```python
import functools

import jax
import jax.numpy as jnp
from jax import lax
from jax.experimental import pallas as pl
from jax.experimental.pallas import tpu as pltpu

D_MODEL = 1024
D_FF = 2816
N_SUB = 3
W_A = D_MODEL // 2
W_B = D_MODEL // 2
H_A = 8
DH_A = W_A // H_A
G_B = 8
DG_B = W_B // G_B
CONV_A = 4
CONV_B = 31
LRU_C = 8.0
W_C = D_MODEL
H_C = 8
DH_C = W_C // H_C
CHUNK = 128
EPS = 1e-6

F32 = jnp.float32
BF16 = jnp.bfloat16

VMEM_LIMIT_BYTES = 56 * 1024 * 1024
SUBLANES = 8
FFN_ROWS = 512
FFN_CHUNK = 256
MIX_ROWS = 512
ADA_COLS = 1536
ADA_ROWS = 16
HALO_A = 8
HALO_B = 32
CONV_ROWS = 64
LANES = 128
CONV_LANES = LANES
SEG_PITCH = 72


def _const_spec(shape):
    nd = len(shape)
    return pl.BlockSpec(shape, lambda *_: (0,) * nd, pipeline_mode=pl.Buffered(1))


def _prenorm(x, mod, pre_g):
    ms = jnp.mean(x * x, axis=-1, keepdims=True)
    return (x * lax.rsqrt(ms + EPS)) * (pre_g * (1.0 + mod[1:2])) + mod[0:1]


def _postnorm_residual(x, y, mod, post_g, res_w):
    ms = jnp.mean(y * y, axis=-1, keepdims=True)
    return x + (y * lax.rsqrt(ms + EPS)) * ((res_w * (1.0 + mod[2:3])) * post_g)


def _dot(a, b):
    return jnp.dot(a, b, preferred_element_type=F32)


def _ada_kernel(c_ref, w_ref, b_ref, o_ref):
    ca = jax.nn.silu(c_ref[...]).astype(BF16)
    o_ref[0] = _dot(ca, w_ref[0].astype(BF16)) + b_ref[0]


def _ada_call(c_pad, ada_w, ada_b):
    depth, d, n_out = ada_w.shape
    return pl.pallas_call(
        _ada_kernel,
        grid=(depth, n_out // ADA_COLS),
        in_specs=[
            pl.BlockSpec((ADA_ROWS, d), lambda l, j: (0, 0)),
            pl.BlockSpec((1, d, ADA_COLS), lambda l, j: (l, 0, j)),
            pl.BlockSpec((1, 1, ADA_COLS), lambda l, j: (l, 0, j)),
        ],
        out_specs=pl.BlockSpec((1, ADA_ROWS, ADA_COLS), lambda l, j: (l, 0, j)),
        out_shape=jax.ShapeDtypeStruct((depth, ADA_ROWS, n_out), F32),
        compiler_params=pltpu.CompilerParams(
            dimension_semantics=("arbitrary", "arbitrary"),
            vmem_limit_bytes=VMEM_LIMIT_BYTES),
        name="ada_proj",
    )(c_pad, ada_w, ada_b.reshape(depth, 1, n_out))


def _ffn_kernel(x_ref, mod_ref, pre_ref, post_ref, w13_ref, w2_ref, o_ref, h_ref, *, res_w):
    mod = mod_ref[0]
    h_ref[...] = _prenorm(x_ref[...], mod, pre_ref[...]).astype(BF16)
    acc = None
    for c in range(D_FF // FFN_CHUNK):
        lo = c * FFN_CHUNK
        h = h_ref[...]
        g = _dot(h, w13_ref[:, lo:lo + FFN_CHUNK])
        u = _dot(h, w13_ref[:, D_FF + lo:D_FF + lo + FFN_CHUNK])
        act = (jax.nn.silu(g) * u).astype(BF16)
        p = _dot(act, w2_ref[lo:lo + FFN_CHUNK, :])
        acc = p if acc is None else acc + p
    o_ref[...] = _postnorm_residual(x_ref[...], acc, mod, post_ref[...], res_w)


def _ffn_call(x2, mod, pre_g, post_g, w13, w2, *, res_w, rows_per_batch):
    n, d = x2.shape
    tiles_per_batch = rows_per_batch // FFN_ROWS
    return pl.pallas_call(
        functools.partial(_ffn_kernel, res_w=res_w),
        grid=(n // FFN_ROWS,),
        in_specs=[
            pl.BlockSpec((FFN_ROWS, d), lambda i: (i, 0)),
            pl.BlockSpec((1, 3, d), lambda i: (i // tiles_per_batch, 0, 0)),
            _const_spec((1, d)),
            _const_spec((1, d)),
            _const_spec(w13.shape),
            _const_spec(w2.shape),
        ],
        out_specs=pl.BlockSpec((FFN_ROWS, d), lambda i: (i, 0)),
        out_shape=jax.ShapeDtypeStruct((n, d), F32),
        scratch_shapes=[pltpu.VMEM((FFN_ROWS, d), BF16)],
        compiler_params=pltpu.CompilerParams(
            dimension_semantics=("arbitrary",),
            vmem_limit_bytes=VMEM_LIMIT_BYTES),
        name="ffn_sublayer",
    )(x2, mod, pre_g, post_g, w13, w2)


def _causal_conv(buf_ref, w_ref, b_ref, out_ref, *, taps, halo, rows):
    base = halo - (taps - 1)
    groups = {}
    for k in range(taps):
        groups.setdefault((base + k) % SUBLANES, []).append(((base + k) // SUBLANES, k))
    n_tiles = CONV_ROWS // SUBLANES
    sub_id = lax.broadcasted_iota(jnp.int32, (n_tiles, SUBLANES, CONV_LANES), 1)
    for l0 in range(0, buf_ref.shape[-1], CONV_LANES):
        lanes = slice(l0, l0 + CONV_LANES)
        for r0 in range(0, rows, CONV_ROWS):
            y = jnp.broadcast_to(b_ref[:, lanes], (CONV_ROWS, CONV_LANES))
            for s, members in sorted(groups.items()):
                ext = CONV_ROWS + (SUBLANES if s else 0)
                z = None
                for q, k in members:
                    lo = r0 + q * SUBLANES
                    term = w_ref[k:k + 1, lanes] * buf_ref[lo:lo + ext, lanes]
                    z = term if z is None else z + term
                if s:
                    head = z[0:CONV_ROWS].reshape(n_tiles, SUBLANES, CONV_LANES)
                    tail = z[SUBLANES:].reshape(n_tiles, SUBLANES, CONV_LANES)
                    z = pltpu.roll(jnp.where(sub_id >= s, head, tail), SUBLANES - s, axis=1)
                    z = z.reshape(CONV_ROWS, CONV_LANES)
                y = y + z
            out_ref[r0:r0 + CONV_ROWS, lanes] = y


def _sublane_scan(c, p, sub_id):
    for d in (1, 2, 4):
        keep = sub_id >= d
        c_prev = jnp.where(keep, pltpu.roll(c, d, 0), 0.0)
        p_prev = jnp.where(keep, pltpu.roll(p, d, 0), 1.0)
        c = c + p * c_prev
        p = p * p_prev
    return c


def _linear_scan(a, u, a_seg, u_seg, h_seg, p_seg, carry_ref, out_ref):
    rows, width = a.shape
    seg = rows // SUBLANES
    n_slabs = width // LANES
    sub_id = lax.broadcasted_iota(jnp.int32, (SUBLANES, LANES), 0)
    for l in range(n_slabs):
        lanes = slice(l * LANES, (l + 1) * LANES)
        for j in range(SUBLANES):
            a_seg[l, j * SEG_PITCH:j * SEG_PITCH + seg, :] = a[j * seg:(j + 1) * seg, lanes]
            u_seg[l, j * SEG_PITCH:j * SEG_PITCH + seg, :] = u[j * seg:(j + 1) * seg, lanes]
    for l in range(n_slabs):
        lanes = slice(l * LANES, (l + 1) * LANES)
        h = jnp.zeros((SUBLANES, LANES), F32)
        p = jnp.ones((SUBLANES, LANES), F32)
        for k in range(seg):
            step = pl.ds(k, SUBLANES, stride=SEG_PITCH)
            a_k = a_seg[l, step, :]
            h = a_k * h + u_seg[l, step, :]
            p = a_k * p
            h_seg[l, step, :] = h
            p_seg[l, step, :] = p
        first = sub_id == 0
        c_in = _sublane_scan(jnp.where(first, carry_ref[:, lanes], pltpu.roll(h, 1, 0)),
                             jnp.where(first, 0.0, pltpu.roll(p, 1, 0)), sub_id)
        end = h + p * c_in
        carry_ref[:, lanes] = jnp.broadcast_to(end[SUBLANES - 1:SUBLANES, :], (SUBLANES, LANES))
        for j in range(SUBLANES):
            rows_j = slice(j * SEG_PITCH, j * SEG_PITCH + seg)
            out_ref[j * seg:(j + 1) * seg, lanes] = (
                h_seg[l, rows_j, :] + p_seg[l, rows_j, :] * c_in[j:j + 1, :])


def _ab_kernel(x_ref, mod_ref, pre_ref, post_ref, win_ref, acw_ref, acb_ref, gw_ref, gb_ref,
               lam_ref, bcw_ref, bcb_ref, ng_ref, nb_ref, avg_ref, wout_ref, o_ref,
               ax_buf, v_buf, t_buf, a_seg, u_seg, h_seg, p_seg, hc_ref):
    rows = x_ref.shape[0]
    j = pl.program_id(1)

    @pl.when(j == 0)
    def _():
        ax_buf[0:HALO_A, :] = jnp.zeros((HALO_A, W_A), F32)
        v_buf[0:HALO_B, :] = jnp.zeros((HALO_B, W_B), F32)
        hc_ref[...] = jnp.zeros_like(hc_ref)

    mod = mod_ref[0]
    h = _prenorm(x_ref[...], mod, pre_ref[...]).astype(BF16)
    z = _dot(h, win_ref[...])
    a_gate = z[:, 0:W_A]
    b_val = z[:, 2 * W_A:2 * W_A + W_B]
    b_gate = z[:, 2 * W_A + W_B:]

    ax_buf[HALO_A:HALO_A + rows, :] = z[:, W_A:2 * W_A]
    _causal_conv(ax_buf, acw_ref, acb_ref, t_buf, taps=CONV_A, halo=HALO_A, rows=rows)
    ax_buf[0:HALO_A, :] = ax_buf[rows:rows + HALO_A, :]
    xr = t_buf[...]
    gates = _dot(xr.astype(BF16), gw_ref[...]) + gb_ref[...]
    r_gate = jax.nn.sigmoid(gates[:, 0:W_A])
    i_gate = jax.nn.sigmoid(gates[:, W_A:])
    log_a = (LRU_C * r_gate) * jax.nn.log_sigmoid(lam_ref[...])
    th = jnp.tanh(log_a)
    _linear_scan(jnp.exp(log_a), jnp.sqrt((-2.0 * th) / (1.0 - th)) * (i_gate * xr),
                 a_seg, u_seg, h_seg, p_seg, hc_ref, t_buf)
    y_a = t_buf[...] * jax.nn.gelu(a_gate)

    v_buf[HALO_B:HALO_B + rows, :] = b_val * jax.nn.sigmoid(b_gate)
    _causal_conv(v_buf, bcw_ref, bcb_ref, t_buf, taps=CONV_B, halo=HALO_B, rows=rows)
    v_buf[0:HALO_B, :] = v_buf[rows:rows + HALO_B, :]
    v = t_buf[...]
    avg = avg_ref[...]

    def group_mean(t):
        hi = t.astype(BF16)
        lo = (t - hi.astype(F32)).astype(BF16)
        return _dot(hi, avg) + _dot(lo, avg)

    cen = v - group_mean(v)
    var = group_mean(cen * cen)
    vn = cen * lax.rsqrt(var + EPS) * ng_ref[...] + nb_ref[...]
    y_b = jax.nn.silu(vn)

    y = _dot(y_a.astype(BF16), wout_ref[0:W_A, :]) + _dot(y_b.astype(BF16), wout_ref[W_A:, :])
    o_ref[...] = _postnorm_residual(x_ref[...], y, mod, post_ref[...], 1.0)


def _ab_call(x2, mod, pre_g, post_g, w_in, a_conv_w, a_conv_b, gate_w, gate_b, lam,
             b_conv_w, b_conv_b, norm_g, norm_b, avg, w_out, *, batch, rows_per_batch):
    n, d = x2.shape
    rows = MIX_ROWS
    tiles = rows_per_batch // rows
    consts = (pre_g, post_g, w_in, a_conv_w, a_conv_b, gate_w, gate_b, lam,
              b_conv_w, b_conv_b, norm_g, norm_b, avg, w_out)
    return pl.pallas_call(
        _ab_kernel,
        grid=(batch, tiles),
        in_specs=[
            pl.BlockSpec((rows, d), lambda b, j: (b * tiles + j, 0)),
            pl.BlockSpec((1, 3, d), lambda b, j: (b, 0, 0)),
        ] + [_const_spec(a.shape) for a in consts],
        out_specs=pl.BlockSpec((rows, d), lambda b, j: (b * tiles + j, 0)),
        out_shape=jax.ShapeDtypeStruct((n, d), F32),
        scratch_shapes=[
            pltpu.VMEM((HALO_A + rows, W_A), F32),
            pltpu.VMEM((HALO_B + rows, W_B), F32),
            pltpu.VMEM((rows, W_A), F32),
        ] + [pltpu.VMEM((W_A // LANES, SUBLANES * SEG_PITCH, LANES), F32)] * 4 + [
            pltpu.VMEM((SUBLANES, W_A), F32),
        ],
        compiler_params=pltpu.CompilerParams(
            dimension_semantics=("arbitrary", "arbitrary"),
            vmem_limit_bytes=VMEM_LIMIT_BYTES),
        name="mixer_ab_sublayer",
    )(x2, mod, *consts)


def _c_kernel(x_ref, mod_ref, pre_ref, post_ref, win_ref, bin_ref, ng_ref, nb_ref,
              ws_ref, bs_ref, wout_ref, o_ref):
    rows = x_ref.shape[0]
    n_chunks = rows // CHUNK
    mod = mod_ref[0]
    h = _prenorm(x_ref[...], mod, pre_ref[...]).astype(BF16)
    z = jax.nn.gelu(_dot(h, win_ref[...]) + bin_ref[...])
    u = z[:, 0:W_C]
    v = z[:, W_C:]
    mu = jnp.mean(v, axis=-1, keepdims=True)
    cen = v - mu
    var = jnp.mean(cen * cen, axis=-1, keepdims=True)
    vb = (cen * lax.rsqrt(var + EPS) * ng_ref[...] + nb_ref[...]).astype(BF16)

    t_id = lax.broadcasted_iota(jnp.int32, (CHUNK, CHUNK), 0)
    s_id = lax.broadcasted_iota(jnp.int32, (CHUNK, CHUNK), 1)
    causal = s_id <= t_id
    per_head = []
    for hd in range(H_C):
        ws = jnp.where(causal, ws_ref[hd], 0.0).astype(BF16)
        c0 = hd * DH_C
        vh = jnp.concatenate(
            [vb[n * CHUNK:(n + 1) * CHUNK, c0:c0 + DH_C] for n in range(n_chunks)], axis=1)
        per_head.append(_dot(ws, vh))
    mixed = jnp.concatenate(
        [jnp.concatenate([per_head[hd][:, n * DH_C:(n + 1) * DH_C] for hd in range(H_C)], axis=1)
         + bs_ref[...] for n in range(n_chunks)], axis=0)
    y = _dot((u * mixed).astype(BF16), wout_ref[...])
    o_ref[...] = _postnorm_residual(x_ref[...], y, mod, post_ref[...], 1.0)


def _c_call(x2, mod, pre_g, post_g, w_in, b_in, norm_g, norm_b, w_s, bias_s, w_out,
            *, rows_per_batch):
    n, d = x2.shape
    rows = MIX_ROWS
    tiles_per_batch = rows_per_batch // rows
    consts = (pre_g, post_g, w_in, b_in, norm_g, norm_b, w_s, bias_s, w_out)
    return pl.pallas_call(
        _c_kernel,
        grid=(n // rows,),
        in_specs=[
            pl.BlockSpec((rows, d), lambda i: (i, 0)),
            pl.BlockSpec((1, 3, d), lambda i: (i // tiles_per_batch, 0, 0)),
        ] + [_const_spec(a.shape) for a in consts],
        out_specs=pl.BlockSpec((rows, d), lambda i: (i, 0)),
        out_shape=jax.ShapeDtypeStruct((n, d), F32),
        compiler_params=pltpu.CompilerParams(
            dimension_semantics=("arbitrary",),
            vmem_limit_bytes=VMEM_LIMIT_BYTES),
        name="mixer_c_sublayer",
    )(x2, mod, *consts)


def _row(v):
    return v.reshape(1, -1)


def _block_diag_gate(gate_w, gate_b):
    eye = jnp.eye(H_A, dtype=gate_w.dtype)
    w_r = jnp.einsum("hde,hg->hdge", gate_w[:, :, :DH_A], eye).reshape(W_A, W_A)
    w_i = jnp.einsum("hde,hg->hdge", gate_w[:, :, DH_A:], eye).reshape(W_A, W_A)
    w = jnp.concatenate([w_r, w_i], axis=1).astype(BF16)
    b = jnp.concatenate([gate_b[:, :DH_A].reshape(-1), gate_b[:, DH_A:].reshape(-1)])
    return w, _row(b)


def kernel(x, c, ada_w, ada_b, norm_pre, norm_post, ffn_w13, ffn_w2, ab_w_in, a_conv_w, a_conv_b, a_gate_w, a_gate_b, a_lam, b_conv_w, b_conv_b, b_norm_g, b_norm_b, ab_w_out, c_w_in, c_b_in, c_norm_g, c_norm_b, c_w_s, c_b_s, c_w_out):
    bsz, seq, d = x.shape
    depth = ada_w.shape[0]
    assert d == D_MODEL and seq % FFN_ROWS == 0 and seq % MIX_ROWS == 0 and bsz <= ADA_ROWS
    x2 = x.reshape(bsz * seq, d)

    c_pad = jnp.zeros((ADA_ROWS, d), F32).at[:bsz].set(c)
    mods = _ada_call(c_pad, ada_w, ada_b)[:, :bsz].reshape(depth, bsz, N_SUB, 3, d)

    group_id = jnp.arange(W_B) // DG_B
    avg = jnp.where(group_id[:, None] == group_id[None, :], 1.0 / DG_B, 0.0).astype(BF16)

    for l in range(depth):
        k = l // 2

        def ffn(x2, s, i, l=l):
            return _ffn_call(
                x2, mods[l, :, s], _row(norm_pre[l, s]), _row(norm_post[l, s]),
                ffn_w13[l, i].astype(BF16), ffn_w2[l, i].astype(BF16),
                res_w=0.5, rows_per_batch=seq)

        x2 = ffn(x2, 0, 0)
        if l % 2 == 0:
            gate_w, gate_b = _block_diag_gate(a_gate_w[k], a_gate_b[k])
            x2 = _ab_call(
                x2, mods[l, :, 1], _row(norm_pre[l, 1]), _row(norm_post[l, 1]),
                ab_w_in[k].astype(BF16), a_conv_w[k], _row(a_conv_b[k]), gate_w, gate_b,
                _row(a_lam[k]), b_conv_w[k], _row(b_conv_b[k]), _row(b_norm_g[k]),
                _row(b_norm_b[k]), avg, ab_w_out[k].astype(BF16),
                batch=bsz, rows_per_batch=seq)
        else:
            bias_s = jnp.repeat(jnp.transpose(c_b_s[k]), DH_C, axis=1)
            x2 = _c_call(
                x2, mods[l, :, 1], _row(norm_pre[l, 1]), _row(norm_post[l, 1]),
                c_w_in[k].astype(BF16), _row(c_b_in[k]), _row(c_norm_g[k]), _row(c_norm_b[k]),
                c_w_s[k], bias_s, c_w_out[k].astype(BF16), rows_per_batch=seq)
        x2 = ffn(x2, 2, 1)
    return x2.reshape(bsz, seq, d)
```

```python
import functools

import jax
import jax.numpy as jnp
from jax import lax
from jax.experimental import pallas as pl
from jax.experimental.pallas import tpu as pltpu

D_MODEL = 1024
D_FF = 2816
N_SUB = 3
W_A = D_MODEL // 2
W_B = D_MODEL // 2
H_A = 8
DH_A = W_A // H_A
G_B = 8
DG_B = W_B // G_B
CONV_A = 4
CONV_B = 31
LRU_C = 8.0
W_C = D_MODEL
H_C = 8
DH_C = W_C // H_C
CHUNK = 128
EPS = 1e-6

F32 = jnp.float32
BF16 = jnp.bfloat16

VMEM_LIMIT_BYTES = 56 * 1024 * 1024
SUBLANES = 8
FFN_ROWS = 1024
FFN_CHUNK = 256
MIX_ROWS = 512
ADA_COLS = 1536
ADA_ROWS = 16
HALO_A = 8
HALO_B = 32
PROJ_CHUNK = 256
CONV_ROWS = 128
LANES = 128
CONV_LANES = LANES
SEG_PITCH = 72


def _const_spec(shape):
    nd = len(shape)
    return pl.BlockSpec(shape, lambda *_: (0,) * nd, pipeline_mode=pl.Buffered(1))


def _prenorm(x, mod, pre_g):
    ms = jnp.mean(x * x, axis=-1, keepdims=True)
    return (x * lax.rsqrt(ms + EPS)) * (pre_g * (1.0 + mod[1:2])) + mod[0:1]


def _postnorm_residual(x, y, mod, post_g, res_w):
    ms = jnp.mean(y * y, axis=-1, keepdims=True)
    return x + (y * lax.rsqrt(ms + EPS)) * ((res_w * (1.0 + mod[2:3])) * post_g)


def _dot(a, b):
    return jnp.dot(a, b, preferred_element_type=F32)


def _ada_kernel(c_ref, w_ref, b_ref, o_ref):
    ca = jax.nn.silu(c_ref[...]).astype(BF16)
    o_ref[0] = _dot(ca, w_ref[0].astype(BF16)) + b_ref[0]


def _ada_call(c_pad, ada_w, ada_b):
    depth, d, n_out = ada_w.shape
    return pl.pallas_call(
        _ada_kernel,
        grid=(depth, n_out // ADA_COLS),
        in_specs=[
            pl.BlockSpec((ADA_ROWS, d), lambda l, j: (0, 0)),
            pl.BlockSpec((1, d, ADA_COLS), lambda l, j: (l, 0, j)),
            pl.BlockSpec((1, 1, ADA_COLS), lambda l, j: (l, 0, j)),
        ],
        out_specs=pl.BlockSpec((1, ADA_ROWS, ADA_COLS), lambda l, j: (l, 0, j)),
        out_shape=jax.ShapeDtypeStruct((depth, ADA_ROWS, n_out), F32),
        compiler_params=pltpu.CompilerParams(
            dimension_semantics=("arbitrary", "arbitrary"),
            vmem_limit_bytes=VMEM_LIMIT_BYTES),
        name="ada_proj",
    )(c_pad, ada_w, ada_b.reshape(depth, 1, n_out))


def _ffn_kernel(x_ref, mod_ref, pre_ref, post_ref, w13_ref, w2_ref, o_ref, h_ref, *, res_w):
    mod = mod_ref[0]
    h_ref[...] = _prenorm(x_ref[...], mod, pre_ref[...]).astype(BF16)
    acc = None
    for c in range(D_FF // FFN_CHUNK):
        lo = c * FFN_CHUNK
        h = h_ref[...]
        g = _dot(h, w13_ref[:, lo:lo + FFN_CHUNK])
        u = _dot(h, w13_ref[:, D_FF + lo:D_FF + lo + FFN_CHUNK])
        act = (jax.nn.silu(g) * u).astype(BF16)
        p = _dot(act, w2_ref[lo:lo + FFN_CHUNK, :])
        acc = p if acc is None else acc + p
    o_ref[...] = _postnorm_residual(x_ref[...], acc, mod, post_ref[...], res_w)


def _ffn_call(x2, mod, pre_g, post_g, w13_all, w2_all, layer, which, *, res_w, rows_per_batch):
    n, d = x2.shape
    tiles_per_batch = rows_per_batch // FFN_ROWS
    single = pl.Buffered(1)
    return pl.pallas_call(
        functools.partial(_ffn_kernel, res_w=res_w),
        grid=(n // FFN_ROWS,),
        in_specs=[
            pl.BlockSpec((FFN_ROWS, d), lambda i: (i, 0)),
            pl.BlockSpec((1, 3, d), lambda i: (i // tiles_per_batch, 0, 0)),
            _const_spec((1, d)),
            _const_spec((1, d)),
            pl.BlockSpec((None, None) + w13_all.shape[2:], lambda i: (layer, which, 0, 0),
                         pipeline_mode=single),
            pl.BlockSpec((None, None) + w2_all.shape[2:], lambda i: (layer, which, 0, 0),
                         pipeline_mode=single),
        ],
        out_specs=pl.BlockSpec((FFN_ROWS, d), lambda i: (i, 0)),
        out_shape=jax.ShapeDtypeStruct((n, d), F32),
        scratch_shapes=[pltpu.VMEM((FFN_ROWS, d), BF16)],
        compiler_params=pltpu.CompilerParams(
            dimension_semantics=("arbitrary",),
            vmem_limit_bytes=VMEM_LIMIT_BYTES),
        name="ffn_sublayer",
    )(x2, mod, pre_g, post_g, w13_all, w2_all)


def _causal_conv(buf_ref, w_ref, b_ref, out_ref, *, taps, halo, rows, slabs=None):
    base = halo - (taps - 1)
    if slabs is None:
        slabs = range(buf_ref.shape[-1] // CONV_LANES)
    groups = {}
    for k in range(taps):
        groups.setdefault((base + k) % SUBLANES, []).append(((base + k) // SUBLANES, k))
    n_tiles = CONV_ROWS // SUBLANES
    sub_id = lax.broadcasted_iota(jnp.int32, (n_tiles, SUBLANES, CONV_LANES), 1)
    for l0 in (sl * CONV_LANES for sl in slabs):
        lanes = slice(l0, l0 + CONV_LANES)
        for r0 in range(0, rows, CONV_ROWS):
            y = jnp.broadcast_to(b_ref[:, lanes], (CONV_ROWS, CONV_LANES))
            for s, members in sorted(groups.items()):
                ext = CONV_ROWS + (SUBLANES if s else 0)
                z = None
                for q, k in members:
                    lo = r0 + q * SUBLANES
                    term = w_ref[k:k + 1, lanes] * buf_ref[lo:lo + ext, lanes]
                    z = term if z is None else z + term
                if s:
                    head = z[0:CONV_ROWS].reshape(n_tiles, SUBLANES, CONV_LANES)
                    tail = z[SUBLANES:].reshape(n_tiles, SUBLANES, CONV_LANES)
                    z = pltpu.roll(jnp.where(sub_id >= s, head, tail), SUBLANES - s, axis=1)
                    z = z.reshape(CONV_ROWS, CONV_LANES)
                y = y + z
            out_ref[r0:r0 + CONV_ROWS, lanes] = y


def _sublane_scan(c, p, sub_id):
    for d in (1, 2, 4):
        keep = sub_id >= d
        c_prev = jnp.where(keep, pltpu.roll(c, d, 0), 0.0)
        p_prev = jnp.where(keep, pltpu.roll(p, d, 0), 1.0)
        c = c + p * c_prev
        p = p * p_prev
    return c


def _linear_scan(a, u, a_seg, u_seg, h_seg, p_seg, carry_ref, out_ref):
    rows, width = a.shape
    seg = rows // SUBLANES
    n_slabs = width // LANES
    sub_id = lax.broadcasted_iota(jnp.int32, (SUBLANES, LANES), 0)
    for l in range(n_slabs):
        lanes = slice(l * LANES, (l + 1) * LANES)
        for j in range(SUBLANES):
            a_seg[l, j * SEG_PITCH:j * SEG_PITCH + seg, :] = a[j * seg:(j + 1) * seg, lanes]
            u_seg[l, j * SEG_PITCH:j * SEG_PITCH + seg, :] = u[j * seg:(j + 1) * seg, lanes]
    for l in range(n_slabs):
        lanes = slice(l * LANES, (l + 1) * LANES)
        h = jnp.zeros((SUBLANES, LANES), F32)
        p = jnp.ones((SUBLANES, LANES), F32)
        for k in range(seg):
            step = pl.ds(k, SUBLANES, stride=SEG_PITCH)
            a_k = a_seg[l, step, :]
            h = a_k * h + u_seg[l, step, :]
            p = a_k * p
            h_seg[l, step, :] = h
            p_seg[l, step, :] = p
        first = sub_id == 0
        c_in = _sublane_scan(jnp.where(first, carry_ref[:, lanes], pltpu.roll(h, 1, 0)),
                             jnp.where(first, 0.0, pltpu.roll(p, 1, 0)), sub_id)
        end = h + p * c_in
        carry_ref[:, lanes] = jnp.broadcast_to(end[SUBLANES - 1:SUBLANES, :], (SUBLANES, LANES))
        for j in range(SUBLANES):
            rows_j = slice(j * SEG_PITCH, j * SEG_PITCH + seg)
            out_ref[j * seg:(j + 1) * seg, lanes] = (
                h_seg[l, rows_j, :] + p_seg[l, rows_j, :] * c_in[j:j + 1, :])


def _ab_kernel(x_ref, mod_ref, pre_ref, post_ref, win_ref, acw_ref, acb_ref, gw_ref, gb_ref,
               lam_ref, bcw_ref, bcb_ref, ng_ref, nb_ref, avg_ref, wout_ref, o_ref,
               h_ref, ax_buf, v_buf, t_buf, vc_buf, a_seg, u_seg, h_seg, p_seg, hc_ref):
    rows = x_ref.shape[0]
    j = pl.program_id(1)

    @pl.when(j == 0)
    def _():
        ax_buf[0:HALO_A, :] = jnp.zeros((HALO_A, W_A), F32)
        v_buf[0:HALO_B, :] = jnp.zeros((HALO_B, W_B), F32)
        hc_ref[...] = jnp.zeros_like(hc_ref)

    mod = mod_ref[0]
    h_ref[...] = _prenorm(x_ref[...], mod, pre_ref[...]).astype(BF16)

    def in_proj(c0, width):
        return _dot(h_ref[...], win_ref[:, c0:c0 + width])

    ax_buf[HALO_A:HALO_A + rows, :] = in_proj(0, W_A)
    _causal_conv(ax_buf, acw_ref, acb_ref, t_buf, taps=CONV_A, halo=HALO_A, rows=rows)
    ax_buf[0:HALO_A, :] = ax_buf[rows:rows + HALO_A, :]
    xr = t_buf[...]
    gates = _dot(xr.astype(BF16), gw_ref[...]) + gb_ref[...]
    r_gate = jax.nn.sigmoid(gates[:, 0:W_A])
    i_gate = jax.nn.sigmoid(gates[:, W_A:])
    log_a = (LRU_C * r_gate) * jax.nn.log_sigmoid(lam_ref[...])
    th = jnp.tanh(log_a)
    _linear_scan(jnp.exp(log_a), jnp.sqrt((-2.0 * th) / (1.0 - th)) * (i_gate * xr),
                 a_seg, u_seg, h_seg, p_seg, hc_ref, t_buf)

    for sl in range(W_B // CONV_LANES):
        zb = in_proj(W_A + 2 * CONV_LANES * sl, 2 * CONV_LANES)
        lanes = slice(sl * CONV_LANES, (sl + 1) * CONV_LANES)
        v_buf[HALO_B:HALO_B + rows, lanes] = (
            zb[:, 0:CONV_LANES] * jax.nn.sigmoid(zb[:, CONV_LANES:]))
        _causal_conv(v_buf, bcw_ref, bcb_ref, vc_buf, taps=CONV_B, halo=HALO_B, rows=rows,
                     slabs=(sl,))
    v_buf[0:HALO_B, :] = v_buf[rows:rows + HALO_B, :]
    y_a = t_buf[...] * jax.nn.gelu(in_proj(W_A + 2 * W_B, W_A))

    v = vc_buf[...]
    avg = avg_ref[...]

    def group_mean(t):
        hi = t.astype(BF16)
        lo = (t - hi.astype(F32)).astype(BF16)
        return _dot(hi, avg) + _dot(lo, avg)

    cen = v - group_mean(v)
    var = group_mean(cen * cen)
    vn = cen * lax.rsqrt(var + EPS) * ng_ref[...] + nb_ref[...]
    y_b = jax.nn.silu(vn)

    y = _dot(y_a.astype(BF16), wout_ref[0:W_A, :]) + _dot(y_b.astype(BF16), wout_ref[W_A:, :])
    o_ref[...] = _postnorm_residual(x_ref[...], y, mod, post_ref[...], 1.0)


def _ab_call(x2, mod, pre_g, post_g, w_in, a_conv_w, a_conv_b, gate_w, gate_b, lam,
             b_conv_w, b_conv_b, norm_g, norm_b, avg, w_out, *, batch, rows_per_batch):
    n, d = x2.shape
    rows = MIX_ROWS
    tiles = rows_per_batch // rows
    consts = (pre_g, post_g, w_in, a_conv_w, a_conv_b, gate_w, gate_b, lam,
              b_conv_w, b_conv_b, norm_g, norm_b, avg, w_out)
    return pl.pallas_call(
        _ab_kernel,
        grid=(batch, tiles),
        in_specs=[
            pl.BlockSpec((rows, d), lambda b, j: (b * tiles + j, 0)),
            pl.BlockSpec((1, 3, d), lambda b, j: (b, 0, 0)),
        ] + [_const_spec(a.shape) for a in consts],
        out_specs=pl.BlockSpec((rows, d), lambda b, j: (b * tiles + j, 0)),
        out_shape=jax.ShapeDtypeStruct((n, d), F32),
        scratch_shapes=[
            pltpu.VMEM((rows, d), BF16),
            pltpu.VMEM((HALO_A + rows, W_A), F32),
            pltpu.VMEM((HALO_B + rows, W_B), F32),
            pltpu.VMEM((rows, W_A), F32),
            pltpu.VMEM((rows, W_B), F32),
        ] + [pltpu.VMEM((W_A // LANES, SUBLANES * SEG_PITCH, LANES), F32)] * 4 + [
            pltpu.VMEM((SUBLANES, W_A), F32),
        ],
        compiler_params=pltpu.CompilerParams(
            dimension_semantics=("arbitrary", "arbitrary"),
            vmem_limit_bytes=VMEM_LIMIT_BYTES),
        name="mixer_ab_sublayer",
    )(x2, mod, *consts)


def _c_kernel(x_ref, mod_ref, pre_ref, post_ref, win_ref, bin_ref, ng_ref, nb_ref,
              ws_ref, bs_ref, wout_ref, o_ref, h_ref, z_buf):
    rows = x_ref.shape[0]
    n_chunks = rows // CHUNK
    mod = mod_ref[0]
    h_ref[...] = _prenorm(x_ref[...], mod, pre_ref[...]).astype(BF16)
    for c0 in list(range(W_C, 2 * W_C, PROJ_CHUNK)) + list(range(0, W_C, PROJ_CHUNK)):
        cols = slice(c0, c0 + PROJ_CHUNK)
        z_buf[:, cols] = jax.nn.gelu(_dot(h_ref[...], win_ref[:, cols]) + bin_ref[:, cols])
    u = z_buf[:, 0:W_C]
    v = z_buf[:, W_C:]
    mu = jnp.mean(v, axis=-1, keepdims=True)
    cen = v - mu
    var = jnp.mean(cen * cen, axis=-1, keepdims=True)
    vb = (cen * lax.rsqrt(var + EPS) * ng_ref[...] + nb_ref[...]).astype(BF16)

    t_id = lax.broadcasted_iota(jnp.int32, (CHUNK, CHUNK), 0)
    s_id = lax.broadcasted_iota(jnp.int32, (CHUNK, CHUNK), 1)
    causal = s_id <= t_id
    per_head = []
    for hd in range(H_C):
        ws = jnp.where(causal, ws_ref[hd], 0.0).astype(BF16)
        c0 = hd * DH_C
        vh = jnp.concatenate(
            [vb[n * CHUNK:(n + 1) * CHUNK, c0:c0 + DH_C] for n in range(n_chunks)], axis=1)
        per_head.append(_dot(ws, vh))
    mixed = jnp.concatenate(
        [jnp.concatenate([per_head[hd][:, n * DH_C:(n + 1) * DH_C] for hd in range(H_C)], axis=1)
         + bs_ref[...] for n in range(n_chunks)], axis=0)
    y = _dot((u * mixed).astype(BF16), wout_ref[...])
    o_ref[...] = _postnorm_residual(x_ref[...], y, mod, post_ref[...], 1.0)


def _c_call(x2, mod, pre_g, post_g, w_in, b_in, norm_g, norm_b, w_s, bias_s, w_out,
            *, rows_per_batch):
    n, d = x2.shape
    rows = MIX_ROWS
    tiles_per_batch = rows_per_batch // rows
    consts = (pre_g, post_g, w_in, b_in, norm_g, norm_b, w_s, bias_s, w_out)
    return pl.pallas_call(
        _c_kernel,
        grid=(n // rows,),
        in_specs=[
            pl.BlockSpec((rows, d), lambda i: (i, 0)),
            pl.BlockSpec((1, 3, d), lambda i: (i // tiles_per_batch, 0, 0)),
        ] + [_const_spec(a.shape) for a in consts],
        out_specs=pl.BlockSpec((rows, d), lambda i: (i, 0)),
        out_shape=jax.ShapeDtypeStruct((n, d), F32),
        scratch_shapes=[pltpu.VMEM((rows, d), BF16), pltpu.VMEM((rows, 2 * W_C), F32)],
        compiler_params=pltpu.CompilerParams(
            dimension_semantics=("arbitrary",),
            vmem_limit_bytes=VMEM_LIMIT_BYTES),
        name="mixer_c_sublayer",
    )(x2, mod, *consts)


def _row(v):
    return v.reshape(1, -1)


def _ab_in_columns(w_in):
    a_gate, a_x, b_val, b_gate = jnp.split(w_in, [W_A, 2 * W_A, 2 * W_A + W_B], axis=1)
    slabs = []
    for l0 in range(0, W_B, CONV_LANES):
        slabs += [b_val[:, l0:l0 + CONV_LANES], b_gate[:, l0:l0 + CONV_LANES]]
    return jnp.concatenate([a_x] + slabs + [a_gate], axis=1)


def _block_diag_gate(gate_w, gate_b):
    eye = jnp.eye(H_A, dtype=gate_w.dtype)
    w_r = jnp.einsum("hde,hg->hdge", gate_w[:, :, :DH_A], eye).reshape(W_A, W_A)
    w_i = jnp.einsum("hde,hg->hdge", gate_w[:, :, DH_A:], eye).reshape(W_A, W_A)
    w = jnp.concatenate([w_r, w_i], axis=1).astype(BF16)
    b = jnp.concatenate([gate_b[:, :DH_A].reshape(-1), gate_b[:, DH_A:].reshape(-1)])
    return w, _row(b)


def kernel(x, c, ada_w, ada_b, norm_pre, norm_post, ffn_w13, ffn_w2, ab_w_in, a_conv_w, a_conv_b, a_gate_w, a_gate_b, a_lam, b_conv_w, b_conv_b, b_norm_g, b_norm_b, ab_w_out, c_w_in, c_b_in, c_norm_g, c_norm_b, c_w_s, c_b_s, c_w_out):
    bsz, seq, d = x.shape
    depth = ada_w.shape[0]
    assert d == D_MODEL and seq % FFN_ROWS == 0 and seq % MIX_ROWS == 0 and bsz <= ADA_ROWS
    x2 = x.reshape(bsz * seq, d)

    c_pad = jnp.zeros((ADA_ROWS, d), F32).at[:bsz].set(c)
    mods = _ada_call(c_pad, ada_w, ada_b)[:, :bsz].reshape(depth, bsz, N_SUB, 3, d)

    group_id = jnp.arange(W_B) // DG_B
    avg = jnp.where(group_id[:, None] == group_id[None, :], 1.0 / DG_B, 0.0).astype(BF16)

    w13_all = ffn_w13.astype(BF16)
    w2_all = ffn_w2.astype(BF16)
    for l in range(depth):
        k = l // 2

        def ffn(x2, s, i, l=l):
            return _ffn_call(
                x2, mods[l, :, s], _row(norm_pre[l, s]), _row(norm_post[l, s]),
                w13_all, w2_all, l, i, res_w=0.5, rows_per_batch=seq)

        x2 = ffn(x2, 0, 0)
        if l % 2 == 0:
            gate_w, gate_b = _block_diag_gate(a_gate_w[k], a_gate_b[k])
            x2 = _ab_call(
                x2, mods[l, :, 1], _row(norm_pre[l, 1]), _row(norm_post[l, 1]),
                _ab_in_columns(ab_w_in[k]).astype(BF16), a_conv_w[k], _row(a_conv_b[k]), gate_w, gate_b,
                _row(a_lam[k]), b_conv_w[k], _row(b_conv_b[k]), _row(b_norm_g[k]),
                _row(b_norm_b[k]), avg, ab_w_out[k].astype(BF16),
                batch=bsz, rows_per_batch=seq)
        else:
            bias_s = jnp.repeat(jnp.transpose(c_b_s[k]), DH_C, axis=1)
            x2 = _c_call(
                x2, mods[l, :, 1], _row(norm_pre[l, 1]), _row(norm_post[l, 1]),
                c_w_in[k].astype(BF16), _row(c_b_in[k]), _row(c_norm_g[k]), _row(c_norm_b[k]),
                c_w_s[k], bias_s, c_w_out[k].astype(BF16), rows_per_batch=seq)
        x2 = ffn(x2, 2, 1)
    return x2.reshape(bsz, seq, d)
```

```python
import functools

import jax
import jax.numpy as jnp
from jax import lax
from jax.experimental import pallas as pl
from jax.experimental.pallas import tpu as pltpu

D_MODEL = 1024
D_FF = 2816
N_SUB = 3
W_A = D_MODEL // 2
W_B = D_MODEL // 2
H_A = 8
DH_A = W_A // H_A
G_B = 8
DG_B = W_B // G_B
CONV_A = 4
CONV_B = 31
LRU_C = 8.0
W_C = D_MODEL
H_C = 8
DH_C = W_C // H_C
CHUNK = 128
EPS = 1e-6

F32 = jnp.float32
BF16 = jnp.bfloat16

VMEM_LIMIT_BYTES = 56 * 1024 * 1024
SUBLANES = 8
FFN_ROWS = 1024
FFN_CHUNK = 256
FFN_ROW_BLOCK = 512
MIX_ROWS = 1024
ADA_COLS = 1536
ADA_ROWS = 16
HALO_A = 8
HALO_B = 32
PROJ_CHUNK = 256
CONV_ROWS = 128
LANES = 128
CONV_LANES = LANES
SEG_PITCH = 136


def _const_spec(shape):
    nd = len(shape)
    return pl.BlockSpec(shape, lambda *_: (0,) * nd, pipeline_mode=pl.Buffered(1))


def _prenorm(x, mod, pre_g):
    ms = jnp.mean(x * x, axis=-1, keepdims=True)
    return (x * lax.rsqrt(ms + EPS)) * (pre_g * (1.0 + mod[1:2])) + mod[0:1]


def _postnorm_residual(x, y, mod, post_g, res_w):
    ms = jnp.mean(y * y, axis=-1, keepdims=True)
    return x + (y * lax.rsqrt(ms + EPS)) * ((res_w * (1.0 + mod[2:3])) * post_g)


def _dot(a, b):
    return jnp.dot(a, b, preferred_element_type=F32)


def _gelu_tanh(x):
    return jax.nn.gelu(x, approximate=True)


def _ada_kernel(c_ref, w_ref, b_ref, o_ref):
    ca = jax.nn.silu(c_ref[...]).astype(BF16)
    o_ref[0] = _dot(ca, w_ref[0].astype(BF16)) + b_ref[0]


def _ada_call(c_pad, ada_w, ada_b):
    depth, d, n_out = ada_w.shape
    return pl.pallas_call(
        _ada_kernel,
        grid=(depth, n_out // ADA_COLS),
        in_specs=[
            pl.BlockSpec((ADA_ROWS, d), lambda l, j: (0, 0)),
            pl.BlockSpec((1, d, ADA_COLS), lambda l, j: (l, 0, j)),
            pl.BlockSpec((1, 1, ADA_COLS), lambda l, j: (l, 0, j)),
        ],
        out_specs=pl.BlockSpec((1, ADA_ROWS, ADA_COLS), lambda l, j: (l, 0, j)),
        out_shape=jax.ShapeDtypeStruct((depth, ADA_ROWS, n_out), F32),
        compiler_params=pltpu.CompilerParams(
            dimension_semantics=("arbitrary", "arbitrary"),
            vmem_limit_bytes=VMEM_LIMIT_BYTES),
        name="ada_proj",
    )(c_pad, ada_w, ada_b.reshape(depth, 1, n_out))


def _ffn_kernel(x_ref, mod_ref, pre_ref, post_ref, w13_ref, w2_ref, o_ref, h_ref, acc_ref, *,
                res_w):
    mod = mod_ref[0]
    pre_g = pre_ref[...]
    post_g = post_ref[...]
    n_chunks = D_FF // FFN_CHUNK
    row_blocks = [slice(r0, r0 + FFN_ROW_BLOCK) for r0 in range(0, x_ref.shape[0], FFN_ROW_BLOCK)]

    def chunk(rows, c):
        lo = c * FFN_CHUNK
        h = h_ref[rows, :]
        g = _dot(h, w13_ref[:, lo:lo + FFN_CHUNK])
        u = _dot(h, w13_ref[:, D_FF + lo:D_FF + lo + FFN_CHUNK])
        act = (jax.nn.silu(g) * u).astype(BF16)
        return _dot(act, w2_ref[lo:lo + FFN_CHUNK, :])

    for rows in row_blocks:
        h_ref[rows, :] = _prenorm(x_ref[rows, :], mod, pre_g).astype(BF16)
        acc_ref[rows, :] = chunk(rows, 0)
    for c in range(1, n_chunks - 1):
        acc_ref[...] += chunk(slice(None), c)
    for rows in row_blocks:
        y = acc_ref[rows, :] + chunk(rows, n_chunks - 1)
        o_ref[rows, :] = _postnorm_residual(x_ref[rows, :], y, mod, post_g, res_w)


def _ffn_call(x2, mod, pre_g, post_g, w13_all, w2_all, layer, which, *, res_w, rows_per_batch):
    n, d = x2.shape
    tiles_per_batch = rows_per_batch // FFN_ROWS
    single = pl.Buffered(1)
    return pl.pallas_call(
        functools.partial(_ffn_kernel, res_w=res_w),
        grid=(n // FFN_ROWS,),
        in_specs=[
            pl.BlockSpec((FFN_ROWS, d), lambda i: (i, 0)),
            pl.BlockSpec((1, 3, d), lambda i: (i // tiles_per_batch, 0, 0)),
            _const_spec((1, d)),
            _const_spec((1, d)),
            pl.BlockSpec((None, None) + w13_all.shape[2:], lambda i: (layer, which, 0, 0),
                         pipeline_mode=single),
            pl.BlockSpec((None, None) + w2_all.shape[2:], lambda i: (layer, which, 0, 0),
                         pipeline_mode=single),
        ],
        out_specs=pl.BlockSpec((FFN_ROWS, d), lambda i: (i, 0)),
        out_shape=jax.ShapeDtypeStruct((n, d), F32),
        scratch_shapes=[pltpu.VMEM((FFN_ROWS, d), BF16), pltpu.VMEM((FFN_ROWS, d), F32)],
        compiler_params=pltpu.CompilerParams(
            dimension_semantics=("arbitrary",),
            vmem_limit_bytes=VMEM_LIMIT_BYTES),
        name="ffn_sublayer",
    )(x2, mod, pre_g, post_g, w13_all, w2_all)


def _causal_conv(buf_ref, w_ref, b_ref, out_ref, *, taps, halo, rows, slabs=None):
    base = halo - (taps - 1)
    if slabs is None:
        slabs = range(buf_ref.shape[-1] // CONV_LANES)
    groups = {}
    for k in range(taps):
        groups.setdefault((base + k) % SUBLANES, []).append(((base + k) // SUBLANES, k))
    n_tiles = CONV_ROWS // SUBLANES
    sub_id = lax.broadcasted_iota(jnp.int32, (n_tiles, SUBLANES, CONV_LANES), 1)
    for l0 in (sl * CONV_LANES for sl in slabs):
        lanes = slice(l0, l0 + CONV_LANES)
        for r0 in range(0, rows, CONV_ROWS):
            y = jnp.broadcast_to(b_ref[:, lanes], (CONV_ROWS, CONV_LANES))
            for s, members in sorted(groups.items()):
                ext = CONV_ROWS + (SUBLANES if s else 0)
                z = None
                for q, k in members:
                    lo = r0 + q * SUBLANES
                    term = w_ref[k:k + 1, lanes] * buf_ref[lo:lo + ext, lanes]
                    z = term if z is None else z + term
                if s:
                    head = z[0:CONV_ROWS].reshape(n_tiles, SUBLANES, CONV_LANES)
                    tail = z[SUBLANES:].reshape(n_tiles, SUBLANES, CONV_LANES)
                    z = pltpu.roll(jnp.where(sub_id >= s, head, tail), SUBLANES - s, axis=1)
                    z = z.reshape(CONV_ROWS, CONV_LANES)
                y = y + z
            out_ref[r0:r0 + CONV_ROWS, lanes] = y


def _sublane_scan(c, p, sub_id):
    for d in (1, 2, 4):
        keep = sub_id >= d
        c_prev = jnp.where(keep, pltpu.roll(c, d, 0), 0.0)
        p_prev = jnp.where(keep, pltpu.roll(p, d, 0), 1.0)
        c = c + p * c_prev
        p = p * p_prev
    return c


def _linear_scan(a, u, a_seg, u_seg, h_seg, p_seg, carry_ref, out_ref):
    rows, width = a.shape
    seg = rows // SUBLANES
    n_slabs = width // LANES
    sub_id = lax.broadcasted_iota(jnp.int32, (SUBLANES, LANES), 0)
    for l in range(n_slabs):
        lanes = slice(l * LANES, (l + 1) * LANES)
        for j in range(SUBLANES):
            a_seg[l, j * SEG_PITCH:j * SEG_PITCH + seg, :] = a[j * seg:(j + 1) * seg, lanes]
            u_seg[l, j * SEG_PITCH:j * SEG_PITCH + seg, :] = u[j * seg:(j + 1) * seg, lanes]
    for l in range(n_slabs):
        lanes = slice(l * LANES, (l + 1) * LANES)
        h = jnp.zeros((SUBLANES, LANES), F32)
        p = jnp.ones((SUBLANES, LANES), F32)
        for k in range(seg):
            step = pl.ds(k, SUBLANES, stride=SEG_PITCH)
            a_k = a_seg[l, step, :]
            h = a_k * h + u_seg[l, step, :]
            p = a_k * p
            h_seg[l, step, :] = h
            p_seg[l, step, :] = p
        first = sub_id == 0
        c_in = _sublane_scan(jnp.where(first, carry_ref[:, lanes], pltpu.roll(h, 1, 0)),
                             jnp.where(first, 0.0, pltpu.roll(p, 1, 0)), sub_id)
        end = h + p * c_in
        carry_ref[:, lanes] = jnp.broadcast_to(end[SUBLANES - 1:SUBLANES, :], (SUBLANES, LANES))
        for j in range(SUBLANES):
            rows_j = slice(j * SEG_PITCH, j * SEG_PITCH + seg)
            out_ref[j * seg:(j + 1) * seg, lanes] = (
                h_seg[l, rows_j, :] + p_seg[l, rows_j, :] * c_in[j:j + 1, :])


def _ab_kernel(x_ref, mod_ref, pre_ref, post_ref, win_ref, acw_ref, acb_ref, gw_ref, gb_ref,
               lam_ref, bcw_ref, bcb_ref, ng_ref, nb_ref, avg_ref, wout_ref, o_ref,
               h_ref, ax_buf, v_buf, t_buf, vc_buf, a_seg, u_seg, h_seg, p_seg, hc_ref):
    rows = x_ref.shape[0]
    j = pl.program_id(1)

    @pl.when(j == 0)
    def _():
        ax_buf[0:HALO_A, :] = jnp.zeros((HALO_A, W_A), F32)
        v_buf[0:HALO_B, :] = jnp.zeros((HALO_B, W_B), F32)
        hc_ref[...] = jnp.zeros_like(hc_ref)

    mod = mod_ref[0]
    h_ref[...] = _prenorm(x_ref[...], mod, pre_ref[...]).astype(BF16)

    def in_proj(c0, width):
        return _dot(h_ref[...], win_ref[:, c0:c0 + width])

    ax_buf[HALO_A:HALO_A + rows, :] = in_proj(0, W_A)
    _causal_conv(ax_buf, acw_ref, acb_ref, t_buf, taps=CONV_A, halo=HALO_A, rows=rows)
    ax_buf[0:HALO_A, :] = ax_buf[rows:rows + HALO_A, :]
    xr = t_buf[...]
    gates = _dot(xr.astype(BF16), gw_ref[...]) + gb_ref[...]
    r_gate = jax.nn.sigmoid(gates[:, 0:W_A])
    i_gate = jax.nn.sigmoid(gates[:, W_A:])
    log_a = (LRU_C * r_gate) * jax.nn.log_sigmoid(lam_ref[...])
    th = jnp.tanh(log_a)
    _linear_scan(jnp.exp(log_a), jnp.sqrt((-2.0 * th) / (1.0 - th)) * (i_gate * xr),
                 a_seg, u_seg, h_seg, p_seg, hc_ref, t_buf)

    for sl in range(W_B // CONV_LANES):
        zb = in_proj(W_A + 2 * CONV_LANES * sl, 2 * CONV_LANES)
        lanes = slice(sl * CONV_LANES, (sl + 1) * CONV_LANES)
        v_buf[HALO_B:HALO_B + rows, lanes] = (
            zb[:, 0:CONV_LANES] * jax.nn.sigmoid(zb[:, CONV_LANES:]))
        _causal_conv(v_buf, bcw_ref, bcb_ref, vc_buf, taps=CONV_B, halo=HALO_B, rows=rows,
                     slabs=(sl,))
    v_buf[0:HALO_B, :] = v_buf[rows:rows + HALO_B, :]
    y_a = t_buf[...] * _gelu_tanh(in_proj(W_A + 2 * W_B, W_A))

    v = vc_buf[...]
    avg = avg_ref[...]

    def group_mean(t):
        hi = t.astype(BF16)
        lo = (t - hi.astype(F32)).astype(BF16)
        return _dot(hi, avg) + _dot(lo, avg)

    cen = v - group_mean(v)
    var = group_mean(cen * cen)
    vn = cen * lax.rsqrt(var + EPS) * ng_ref[...] + nb_ref[...]
    y_b = jax.nn.silu(vn)

    y = _dot(y_a.astype(BF16), wout_ref[0:W_A, :]) + _dot(y_b.astype(BF16), wout_ref[W_A:, :])
    o_ref[...] = _postnorm_residual(x_ref[...], y, mod, post_ref[...], 1.0)


def _ab_call(x2, mod, pre_g, post_g, w_in, a_conv_w, a_conv_b, gate_w, gate_b, lam,
             b_conv_w, b_conv_b, norm_g, norm_b, avg, w_out, *, batch, rows_per_batch):
    n, d = x2.shape
    rows = MIX_ROWS
    tiles = rows_per_batch // rows
    consts = (pre_g, post_g, w_in, a_conv_w, a_conv_b, gate_w, gate_b, lam,
              b_conv_w, b_conv_b, norm_g, norm_b, avg, w_out)
    return pl.pallas_call(
        _ab_kernel,
        grid=(batch, tiles),
        in_specs=[
            pl.BlockSpec((rows, d), lambda b, j: (b * tiles + j, 0)),
            pl.BlockSpec((1, 3, d), lambda b, j: (b, 0, 0)),
        ] + [_const_spec(a.shape) for a in consts],
        out_specs=pl.BlockSpec((rows, d), lambda b, j: (b * tiles + j, 0)),
        out_shape=jax.ShapeDtypeStruct((n, d), F32),
        scratch_shapes=[
            pltpu.VMEM((rows, d), BF16),
            pltpu.VMEM((HALO_A + rows, W_A), F32),
            pltpu.VMEM((HALO_B + rows, W_B), F32),
            pltpu.VMEM((rows, W_A), F32),
            pltpu.VMEM((rows, W_B), F32),
        ] + [pltpu.VMEM((W_A // LANES, SUBLANES * SEG_PITCH, LANES), F32)] * 4 + [
            pltpu.VMEM((SUBLANES, W_A), F32),
        ],
        compiler_params=pltpu.CompilerParams(
            dimension_semantics=("arbitrary", "arbitrary"),
            vmem_limit_bytes=VMEM_LIMIT_BYTES),
        name="mixer_ab_sublayer",
    )(x2, mod, *consts)


def _c_kernel(x_ref, mod_ref, pre_ref, post_ref, win_ref, bin_ref, ng_ref, nb_ref,
              ws_ref, bs_ref, wout_ref, o_ref, h_ref, z_buf):
    rows = x_ref.shape[0]
    n_chunks = rows // CHUNK
    mod = mod_ref[0]
    h_ref[...] = _prenorm(x_ref[...], mod, pre_ref[...]).astype(BF16)
    for c0 in list(range(W_C, 2 * W_C, PROJ_CHUNK)) + list(range(0, W_C, PROJ_CHUNK)):
        cols = slice(c0, c0 + PROJ_CHUNK)
        z_buf[:, cols] = _gelu_tanh(_dot(h_ref[...], win_ref[:, cols]) + bin_ref[:, cols])
    u = z_buf[:, 0:W_C]
    v = z_buf[:, W_C:]
    mu = jnp.mean(v, axis=-1, keepdims=True)
    cen = v - mu
    var = jnp.mean(cen * cen, axis=-1, keepdims=True)
    vb = (cen * lax.rsqrt(var + EPS) * ng_ref[...] + nb_ref[...]).astype(BF16)

    t_id = lax.broadcasted_iota(jnp.int32, (CHUNK, CHUNK), 0)
    s_id = lax.broadcasted_iota(jnp.int32, (CHUNK, CHUNK), 1)
    causal = s_id <= t_id
    per_head = []
    for hd in range(H_C):
        ws = jnp.where(causal, ws_ref[hd], 0.0).astype(BF16)
        c0 = hd * DH_C
        vh = jnp.concatenate(
            [vb[n * CHUNK:(n + 1) * CHUNK, c0:c0 + DH_C] for n in range(n_chunks)], axis=1)
        per_head.append(_dot(ws, vh))
    mixed = jnp.concatenate(
        [jnp.concatenate([per_head[hd][:, n * DH_C:(n + 1) * DH_C] for hd in range(H_C)], axis=1)
         + bs_ref[...] for n in range(n_chunks)], axis=0)
    y = _dot((u * mixed).astype(BF16), wout_ref[...])
    o_ref[...] = _postnorm_residual(x_ref[...], y, mod, post_ref[...], 1.0)


def _c_call(x2, mod, pre_g, post_g, w_in, b_in, norm_g, norm_b, w_s, bias_s, w_out,
            *, rows_per_batch):
    n, d = x2.shape
    rows = MIX_ROWS
    tiles_per_batch = rows_per_batch // rows
    consts = (pre_g, post_g, w_in, b_in, norm_g, norm_b, w_s, bias_s, w_out)
    return pl.pallas_call(
        _c_kernel,
        grid=(n // rows,),
        in_specs=[
            pl.BlockSpec((rows, d), lambda i: (i, 0)),
            pl.BlockSpec((1, 3, d), lambda i: (i // tiles_per_batch, 0, 0)),
        ] + [_const_spec(a.shape) for a in consts],
        out_specs=pl.BlockSpec((rows, d), lambda i: (i, 0)),
        out_shape=jax.ShapeDtypeStruct((n, d), F32),
        scratch_shapes=[pltpu.VMEM((rows, d), BF16), pltpu.VMEM((rows, 2 * W_C), F32)],
        compiler_params=pltpu.CompilerParams(
            dimension_semantics=("arbitrary",),
            vmem_limit_bytes=VMEM_LIMIT_BYTES),
        name="mixer_c_sublayer",
    )(x2, mod, *consts)


def _row(v):
    return v.reshape(1, -1)


def _ab_in_columns(w_in):
    a_gate, a_x, b_val, b_gate = jnp.split(w_in, [W_A, 2 * W_A, 2 * W_A + W_B], axis=1)
    slabs = []
    for l0 in range(0, W_B, CONV_LANES):
        slabs += [b_val[:, l0:l0 + CONV_LANES], b_gate[:, l0:l0 + CONV_LANES]]
    return jnp.concatenate([a_x] + slabs + [a_gate], axis=1)


def _block_diag_gate(gate_w, gate_b):
    eye = jnp.eye(H_A, dtype=gate_w.dtype)
    w_r = jnp.einsum("hde,hg->hdge", gate_w[:, :, :DH_A], eye).reshape(W_A, W_A)
    w_i = jnp.einsum("hde,hg->hdge", gate_w[:, :, DH_A:], eye).reshape(W_A, W_A)
    w = jnp.concatenate([w_r, w_i], axis=1).astype(BF16)
    b = jnp.concatenate([gate_b[:, :DH_A].reshape(-1), gate_b[:, DH_A:].reshape(-1)])
    return w, _row(b)


def kernel(x, c, ada_w, ada_b, norm_pre, norm_post, ffn_w13, ffn_w2, ab_w_in, a_conv_w, a_conv_b, a_gate_w, a_gate_b, a_lam, b_conv_w, b_conv_b, b_norm_g, b_norm_b, ab_w_out, c_w_in, c_b_in, c_norm_g, c_norm_b, c_w_s, c_b_s, c_w_out):
    bsz, seq, d = x.shape
    depth = ada_w.shape[0]
    assert d == D_MODEL and seq % FFN_ROWS == 0 and seq % MIX_ROWS == 0 and bsz <= ADA_ROWS
    x2 = x.reshape(bsz * seq, d)

    c_pad = jnp.zeros((ADA_ROWS, d), F32).at[:bsz].set(c)
    mods = _ada_call(c_pad, ada_w, ada_b)[:, :bsz].reshape(depth, bsz, N_SUB, 3, d)

    group_id = jnp.arange(W_B) // DG_B
    avg = jnp.where(group_id[:, None] == group_id[None, :], 1.0 / DG_B, 0.0).astype(BF16)

    w13_all = ffn_w13.astype(BF16)
    w2_all = ffn_w2.astype(BF16)
    for l in range(depth):
        k = l // 2

        def ffn(x2, s, i, l=l):
            return _ffn_call(
                x2, mods[l, :, s], _row(norm_pre[l, s]), _row(norm_post[l, s]),
                w13_all, w2_all, l, i, res_w=0.5, rows_per_batch=seq)

        x2 = ffn(x2, 0, 0)
        if l % 2 == 0:
            gate_w, gate_b = _block_diag_gate(a_gate_w[k], a_gate_b[k])
            x2 = _ab_call(
                x2, mods[l, :, 1], _row(norm_pre[l, 1]), _row(norm_post[l, 1]),
                _ab_in_columns(ab_w_in[k]).astype(BF16), a_conv_w[k], _row(a_conv_b[k]), gate_w, gate_b,
                _row(a_lam[k]), b_conv_w[k], _row(b_conv_b[k]), _row(b_norm_g[k]),
                _row(b_norm_b[k]), avg, ab_w_out[k].astype(BF16),
                batch=bsz, rows_per_batch=seq)
        else:
            bias_s = jnp.repeat(jnp.transpose(c_b_s[k]), DH_C, axis=1)
            x2 = _c_call(
                x2, mods[l, :, 1], _row(norm_pre[l, 1]), _row(norm_post[l, 1]),
                c_w_in[k].astype(BF16), _row(c_b_in[k]), _row(c_norm_g[k]), _row(c_norm_b[k]),
                c_w_s[k], bias_s, c_w_out[k].astype(BF16), rows_per_batch=seq)
        x2 = ffn(x2, 2, 1)
    return x2.reshape(bsz, seq, d)
```

```python
import functools

import jax
import jax.numpy as jnp
from jax import lax
from jax.experimental import pallas as pl
from jax.experimental.pallas import tpu as pltpu

D_MODEL = 1024
D_FF = 2816
N_SUB = 3
W_A = D_MODEL // 2
W_B = D_MODEL // 2
H_A = 8
DH_A = W_A // H_A
G_B = 8
DG_B = W_B // G_B
CONV_A = 4
CONV_B = 31
LRU_C = 8.0
W_C = D_MODEL
H_C = 8
DH_C = W_C // H_C
CHUNK = 128
EPS = 1e-6

F32 = jnp.float32
BF16 = jnp.bfloat16

VMEM_LIMIT_BYTES = 56 * 1024 * 1024
SUBLANES = 8
FFN_ROWS = 1024
FFN_CHUNK = 256
FFN_ROW_BLOCK = 512
MIX_ROWS = 1024
ADA_COLS = 1536
ADA_ROWS = 16
HALO_A = 8
HALO_B = 32
PROJ_CHUNK = 256
CONV_ROWS = 128
LANES = 128
CONV_LANES = LANES
SEG_PITCH = 136


def _const_spec(shape):
    nd = len(shape)
    return pl.BlockSpec(shape, lambda *_: (0,) * nd, pipeline_mode=pl.Buffered(1))


def _prenorm(x, mod, pre_g):
    ms = jnp.mean(x * x, axis=-1, keepdims=True)
    return (x * lax.rsqrt(ms + EPS)) * (pre_g * (1.0 + mod[1:2])) + mod[0:1]


def _postnorm_residual(x, y, mod, post_g, res_w):
    ms = jnp.mean(y * y, axis=-1, keepdims=True)
    return x + (y * lax.rsqrt(ms + EPS)) * ((res_w * (1.0 + mod[2:3])) * post_g)


def _dot(a, b):
    return jnp.dot(a, b, preferred_element_type=F32)


def _gelu_tanh(x):
    return jax.nn.gelu(x, approximate=True)


def _ada_kernel(c_ref, w_ref, b_ref, o_ref):
    ca = jax.nn.silu(c_ref[...]).astype(BF16)
    o_ref[0] = _dot(ca, w_ref[0].astype(BF16)) + b_ref[0]


def _ada_call(c_pad, ada_w, ada_b):
    depth, d, n_out = ada_w.shape
    return pl.pallas_call(
        _ada_kernel,
        grid=(depth, n_out // ADA_COLS),
        in_specs=[
            pl.BlockSpec((ADA_ROWS, d), lambda l, j: (0, 0)),
            pl.BlockSpec((1, d, ADA_COLS), lambda l, j: (l, 0, j)),
            pl.BlockSpec((1, 1, ADA_COLS), lambda l, j: (l, 0, j)),
        ],
        out_specs=pl.BlockSpec((1, ADA_ROWS, ADA_COLS), lambda l, j: (l, 0, j)),
        out_shape=jax.ShapeDtypeStruct((depth, ADA_ROWS, n_out), F32),
        compiler_params=pltpu.CompilerParams(
            dimension_semantics=("arbitrary", "arbitrary"),
            vmem_limit_bytes=VMEM_LIMIT_BYTES),
        name="ada_proj",
    )(c_pad, ada_w, ada_b.reshape(depth, 1, n_out))


def _ffn_kernel(x_ref, mod_ref, pre_ref, post_ref, w13_ref, w2_ref, o_ref, h_ref, acc_ref, *,
                res_w):
    mod = mod_ref[0]
    pre_g = pre_ref[...]
    post_g = post_ref[...]
    n_chunks = D_FF // FFN_CHUNK
    row_blocks = [slice(r0, r0 + FFN_ROW_BLOCK) for r0 in range(0, x_ref.shape[0], FFN_ROW_BLOCK)]

    def chunk(rows, c):
        lo = c * FFN_CHUNK
        h = h_ref[rows, :]
        g = _dot(h, w13_ref[:, lo:lo + FFN_CHUNK])
        u = _dot(h, w13_ref[:, D_FF + lo:D_FF + lo + FFN_CHUNK])
        act = (jax.nn.silu(g) * u).astype(BF16)
        return _dot(act, w2_ref[lo:lo + FFN_CHUNK, :])

    for rows in row_blocks:
        h_ref[rows, :] = _prenorm(x_ref[rows, :], mod, pre_g).astype(BF16)
        acc_ref[rows, :] = chunk(rows, 0)
    for c in range(1, n_chunks - 1):
        acc_ref[...] += chunk(slice(None), c)
    for rows in row_blocks:
        y = acc_ref[rows, :] + chunk(rows, n_chunks - 1)
        o_ref[rows, :] = _postnorm_residual(x_ref[rows, :], y, mod, post_g, res_w)


def _ffn_call(x2, mod, pre_g, post_g, w13_all, w2_all, layer, which, *, res_w, rows_per_batch):
    n, d = x2.shape
    tiles_per_batch = rows_per_batch // FFN_ROWS
    single = pl.Buffered(1)
    return pl.pallas_call(
        functools.partial(_ffn_kernel, res_w=res_w),
        grid=(n // FFN_ROWS,),
        in_specs=[
            pl.BlockSpec((FFN_ROWS, d), lambda i: (i, 0)),
            pl.BlockSpec((1, 3, d), lambda i: (i // tiles_per_batch, 0, 0)),
            _const_spec((1, d)),
            _const_spec((1, d)),
            pl.BlockSpec((None, None) + w13_all.shape[2:], lambda i: (layer, which, 0, 0),
                         pipeline_mode=single),
            pl.BlockSpec((None, None) + w2_all.shape[2:], lambda i: (layer, which, 0, 0),
                         pipeline_mode=single),
        ],
        out_specs=pl.BlockSpec((FFN_ROWS, d), lambda i: (i, 0)),
        out_shape=jax.ShapeDtypeStruct((n, d), F32),
        scratch_shapes=[pltpu.VMEM((FFN_ROWS, d), BF16), pltpu.VMEM((FFN_ROWS, d), F32)],
        compiler_params=pltpu.CompilerParams(
            dimension_semantics=("arbitrary",),
            vmem_limit_bytes=VMEM_LIMIT_BYTES),
        name="ffn_sublayer",
    )(x2, mod, pre_g, post_g, w13_all, w2_all)


def _causal_conv(buf_ref, w_ref, b_ref, out_ref, *, taps, halo, rows, slabs=None):
    base = halo - (taps - 1)
    if slabs is None:
        slabs = range(buf_ref.shape[-1] // CONV_LANES)
    groups = {}
    for k in range(taps):
        groups.setdefault((base + k) % SUBLANES, []).append(((base + k) // SUBLANES, k))
    n_tiles = CONV_ROWS // SUBLANES
    sub_id = lax.broadcasted_iota(jnp.int32, (n_tiles, SUBLANES, CONV_LANES), 1)
    for l0 in (sl * CONV_LANES for sl in slabs):
        lanes = slice(l0, l0 + CONV_LANES)
        for r0 in range(0, rows, CONV_ROWS):
            y = jnp.broadcast_to(b_ref[:, lanes], (CONV_ROWS, CONV_LANES))
            for s, members in sorted(groups.items()):
                ext = CONV_ROWS + (SUBLANES if s else 0)
                z = None
                for q, k in members:
                    lo = r0 + q * SUBLANES
                    term = w_ref[k:k + 1, lanes] * buf_ref[lo:lo + ext, lanes]
                    z = term if z is None else z + term
                if s:
                    head = z[0:CONV_ROWS].reshape(n_tiles, SUBLANES, CONV_LANES)
                    tail = z[SUBLANES:].reshape(n_tiles, SUBLANES, CONV_LANES)
                    z = pltpu.roll(jnp.where(sub_id >= s, head, tail), SUBLANES - s, axis=1)
                    z = z.reshape(CONV_ROWS, CONV_LANES)
                y = y + z
            out_ref[r0:r0 + CONV_ROWS, lanes] = y


def _sublane_scan(c, p, sub_id):
    for d in (1, 2, 4):
        keep = sub_id >= d
        c_prev = jnp.where(keep, pltpu.roll(c, d, 0), 0.0)
        p_prev = jnp.where(keep, pltpu.roll(p, d, 0), 1.0)
        c = c + p * c_prev
        p = p * p_prev
    return c


def _linear_scan(a, u, a_seg, u_seg, h_seg, p_seg, carry_ref, out_ref):
    rows, width = a.shape
    seg = rows // SUBLANES
    n_slabs = width // LANES
    sub_id = lax.broadcasted_iota(jnp.int32, (SUBLANES, LANES), 0)
    for l in range(n_slabs):
        lanes = slice(l * LANES, (l + 1) * LANES)
        for j in range(SUBLANES):
            a_seg[l, j * SEG_PITCH:j * SEG_PITCH + seg, :] = a[j * seg:(j + 1) * seg, lanes]
            u_seg[l, j * SEG_PITCH:j * SEG_PITCH + seg, :] = u[j * seg:(j + 1) * seg, lanes]
    for l in range(n_slabs):
        lanes = slice(l * LANES, (l + 1) * LANES)
        h = jnp.zeros((SUBLANES, LANES), F32)
        p = jnp.ones((SUBLANES, LANES), F32)
        for k in range(seg):
            step = pl.ds(k, SUBLANES, stride=SEG_PITCH)
            a_k = a_seg[l, step, :]
            h = a_k * h + u_seg[l, step, :]
            p = a_k * p
            h_seg[l, step, :] = h
            p_seg[l, step, :] = p
        first = sub_id == 0
        c_in = _sublane_scan(jnp.where(first, carry_ref[:, lanes], pltpu.roll(h, 1, 0)),
                             jnp.where(first, 0.0, pltpu.roll(p, 1, 0)), sub_id)
        end = h + p * c_in
        carry_ref[:, lanes] = jnp.broadcast_to(end[SUBLANES - 1:SUBLANES, :], (SUBLANES, LANES))
        for j in range(SUBLANES):
            rows_j = slice(j * SEG_PITCH, j * SEG_PITCH + seg)
            out_ref[j * seg:(j + 1) * seg, lanes] = (
                h_seg[l, rows_j, :] + p_seg[l, rows_j, :] * c_in[j:j + 1, :])


def _ab_kernel(x_ref, mod_ref, pre_ref, post_ref, win_ref, acw_ref, acb_ref, gw_ref, gb_ref,
               lam_ref, bcw_ref, bcb_ref, ng_ref, nb_ref, avg_ref, wout_ref, o_ref,
               h_ref, ax_buf, v_buf, t_buf, vc_buf, ya_ref, a_seg, u_seg, h_seg, p_seg, hc_ref):
    rows = x_ref.shape[0]
    j = pl.program_id(1)

    @pl.when(j == 0)
    def _():
        ax_buf[0:HALO_A, :] = jnp.zeros((HALO_A, W_A), F32)
        v_buf[0:HALO_B, :] = jnp.zeros((HALO_B, W_B), F32)
        hc_ref[...] = jnp.zeros_like(hc_ref)

    mod = mod_ref[0]
    h_ref[...] = _prenorm(x_ref[...], mod, pre_ref[...]).astype(BF16)

    def in_proj(c0, width):
        return _dot(h_ref[...], win_ref[:, c0:c0 + width])

    def glu_slab(sl):
        zb = in_proj(W_A + 2 * CONV_LANES * sl, 2 * CONV_LANES)
        lanes = slice(sl * CONV_LANES, (sl + 1) * CONV_LANES)
        v_buf[HALO_B:HALO_B + rows, lanes] = (
            zb[:, 0:CONV_LANES] * jax.nn.sigmoid(zb[:, CONV_LANES:]))

    ax_buf[HALO_A:HALO_A + rows, :] = in_proj(0, W_A)
    glu_slab(0)
    _causal_conv(ax_buf, acw_ref, acb_ref, t_buf, taps=CONV_A, halo=HALO_A, rows=rows)
    ax_buf[0:HALO_A, :] = ax_buf[rows:rows + HALO_A, :]
    glu_slab(1)
    xr = t_buf[...]
    gates = _dot(xr.astype(BF16), gw_ref[...]) + gb_ref[...]
    r_gate = jax.nn.sigmoid(gates[:, 0:W_A])
    i_gate = jax.nn.sigmoid(gates[:, W_A:])
    log_a = (LRU_C * r_gate) * jax.nn.log_sigmoid(lam_ref[...])
    th = jnp.tanh(log_a)
    a_t = jnp.exp(log_a)
    u_t = jnp.sqrt((-2.0 * th) / (1.0 - th)) * (i_gate * xr)
    glu_slab(2)
    _linear_scan(a_t, u_t, a_seg, u_seg, h_seg, p_seg, hc_ref, t_buf)
    glu_slab(3)
    y_a = t_buf[...] * _gelu_tanh(in_proj(W_A + 2 * W_B, W_A))
    ya_ref[...] = _dot(y_a.astype(BF16), wout_ref[0:W_A, :])

    for sl in range(W_B // CONV_LANES):
        _causal_conv(v_buf, bcw_ref, bcb_ref, vc_buf, taps=CONV_B, halo=HALO_B, rows=rows,
                     slabs=(sl,))
    v_buf[0:HALO_B, :] = v_buf[rows:rows + HALO_B, :]

    v = vc_buf[...]
    avg = avg_ref[...]

    def group_mean(t):
        hi = t.astype(BF16)
        lo = (t - hi.astype(F32)).astype(BF16)
        return _dot(hi, avg) + _dot(lo, avg)

    cen = v - group_mean(v)
    var = group_mean(cen * cen)
    vn = cen * lax.rsqrt(var + EPS) * ng_ref[...] + nb_ref[...]
    y_b = jax.nn.silu(vn)

    y = ya_ref[...] + _dot(y_b.astype(BF16), wout_ref[W_A:, :])
    o_ref[...] = _postnorm_residual(x_ref[...], y, mod, post_ref[...], 1.0)


def _ab_call(x2, mod, pre_g, post_g, w_in, a_conv_w, a_conv_b, gate_w, gate_b, lam,
             b_conv_w, b_conv_b, norm_g, norm_b, avg, w_out, *, batch, rows_per_batch):
    n, d = x2.shape
    rows = MIX_ROWS
    tiles = rows_per_batch // rows
    consts = (pre_g, post_g, w_in, a_conv_w, a_conv_b, gate_w, gate_b, lam,
              b_conv_w, b_conv_b, norm_g, norm_b, avg, w_out)
    return pl.pallas_call(
        _ab_kernel,
        grid=(batch, tiles),
        in_specs=[
            pl.BlockSpec((rows, d), lambda b, j: (b * tiles + j, 0)),
            pl.BlockSpec((1, 3, d), lambda b, j: (b, 0, 0)),
        ] + [_const_spec(a.shape) for a in consts],
        out_specs=pl.BlockSpec((rows, d), lambda b, j: (b * tiles + j, 0)),
        out_shape=jax.ShapeDtypeStruct((n, d), F32),
        scratch_shapes=[
            pltpu.VMEM((rows, d), BF16),
            pltpu.VMEM((HALO_A + rows, W_A), F32),
            pltpu.VMEM((HALO_B + rows, W_B), F32),
            pltpu.VMEM((rows, W_A), F32),
            pltpu.VMEM((rows, W_B), F32),
            pltpu.VMEM((rows, d), F32),
        ] + [pltpu.VMEM((W_A // LANES, SUBLANES * SEG_PITCH, LANES), F32)] * 4 + [
            pltpu.VMEM((SUBLANES, W_A), F32),
        ],
        compiler_params=pltpu.CompilerParams(
            dimension_semantics=("arbitrary", "arbitrary"),
            vmem_limit_bytes=VMEM_LIMIT_BYTES),
        name="mixer_ab_sublayer",
    )(x2, mod, *consts)


def _c_kernel(x_ref, mod_ref, pre_ref, post_ref, win_ref, bin_ref, ng_ref, nb_ref,
              ws_ref, bs_ref, wout_ref, o_ref, h_ref, z_buf):
    rows = x_ref.shape[0]
    n_chunks = rows // CHUNK
    mod = mod_ref[0]
    h_ref[...] = _prenorm(x_ref[...], mod, pre_ref[...]).astype(BF16)
    for c0 in list(range(W_C, 2 * W_C, PROJ_CHUNK)) + list(range(0, W_C, PROJ_CHUNK)):
        cols = slice(c0, c0 + PROJ_CHUNK)
        z_buf[:, cols] = _gelu_tanh(_dot(h_ref[...], win_ref[:, cols]) + bin_ref[:, cols])
    u = z_buf[:, 0:W_C]
    v = z_buf[:, W_C:]
    mu = jnp.mean(v, axis=-1, keepdims=True)
    cen = v - mu
    var = jnp.mean(cen * cen, axis=-1, keepdims=True)
    vb = (cen * lax.rsqrt(var + EPS) * ng_ref[...] + nb_ref[...]).astype(BF16)

    t_id = lax.broadcasted_iota(jnp.int32, (CHUNK, CHUNK), 0)
    s_id = lax.broadcasted_iota(jnp.int32, (CHUNK, CHUNK), 1)
    causal = s_id <= t_id
    per_head = []
    for hd in range(H_C):
        ws = jnp.where(causal, ws_ref[hd], 0.0).astype(BF16)
        c0 = hd * DH_C
        vh = jnp.concatenate(
            [vb[n * CHUNK:(n + 1) * CHUNK, c0:c0 + DH_C] for n in range(n_chunks)], axis=1)
        per_head.append(_dot(ws, vh))
    mixed = jnp.concatenate(
        [jnp.concatenate([per_head[hd][:, n * DH_C:(n + 1) * DH_C] for hd in range(H_C)], axis=1)
         + bs_ref[...] for n in range(n_chunks)], axis=0)
    y = _dot((u * mixed).astype(BF16), wout_ref[...])
    o_ref[...] = _postnorm_residual(x_ref[...], y, mod, post_ref[...], 1.0)


def _c_call(x2, mod, pre_g, post_g, w_in, b_in, norm_g, norm_b, w_s, bias_s, w_out,
            *, rows_per_batch):
    n, d = x2.shape
    rows = MIX_ROWS
    tiles_per_batch = rows_per_batch // rows
    consts = (pre_g, post_g, w_in, b_in, norm_g, norm_b, w_s, bias_s, w_out)
    return pl.pallas_call(
        _c_kernel,
        grid=(n // rows,),
        in_specs=[
            pl.BlockSpec((rows, d), lambda i: (i, 0)),
            pl.BlockSpec((1, 3, d), lambda i: (i // tiles_per_batch, 0, 0)),
        ] + [_const_spec(a.shape) for a in consts],
        out_specs=pl.BlockSpec((rows, d), lambda i: (i, 0)),
        out_shape=jax.ShapeDtypeStruct((n, d), F32),
        scratch_shapes=[pltpu.VMEM((rows, d), BF16), pltpu.VMEM((rows, 2 * W_C), F32)],
        compiler_params=pltpu.CompilerParams(
            dimension_semantics=("arbitrary",),
            vmem_limit_bytes=VMEM_LIMIT_BYTES),
        name="mixer_c_sublayer",
    )(x2, mod, *consts)


def _row(v):
    return v.reshape(1, -1)


def _ab_in_columns(w_in):
    a_gate, a_x, b_val, b_gate = jnp.split(w_in, [W_A, 2 * W_A, 2 * W_A + W_B], axis=1)
    slabs = []
    for l0 in range(0, W_B, CONV_LANES):
        slabs += [b_val[:, l0:l0 + CONV_LANES], b_gate[:, l0:l0 + CONV_LANES]]
    return jnp.concatenate([a_x] + slabs + [a_gate], axis=1)


def _block_diag_gate(gate_w, gate_b):
    eye = jnp.eye(H_A, dtype=gate_w.dtype)
    w_r = jnp.einsum("hde,hg->hdge", gate_w[:, :, :DH_A], eye).reshape(W_A, W_A)
    w_i = jnp.einsum("hde,hg->hdge", gate_w[:, :, DH_A:], eye).reshape(W_A, W_A)
    w = jnp.concatenate([w_r, w_i], axis=1).astype(BF16)
    b = jnp.concatenate([gate_b[:, :DH_A].reshape(-1), gate_b[:, DH_A:].reshape(-1)])
    return w, _row(b)


def kernel(x, c, ada_w, ada_b, norm_pre, norm_post, ffn_w13, ffn_w2, ab_w_in, a_conv_w, a_conv_b, a_gate_w, a_gate_b, a_lam, b_conv_w, b_conv_b, b_norm_g, b_norm_b, ab_w_out, c_w_in, c_b_in, c_norm_g, c_norm_b, c_w_s, c_b_s, c_w_out):
    bsz, seq, d = x.shape
    depth = ada_w.shape[0]
    assert d == D_MODEL and seq % FFN_ROWS == 0 and seq % MIX_ROWS == 0 and bsz <= ADA_ROWS
    x2 = x.reshape(bsz * seq, d)

    c_pad = jnp.zeros((ADA_ROWS, d), F32).at[:bsz].set(c)
    mods = _ada_call(c_pad, ada_w, ada_b)[:, :bsz].reshape(depth, bsz, N_SUB, 3, d)

    group_id = jnp.arange(W_B) // DG_B
    avg = jnp.where(group_id[:, None] == group_id[None, :], 1.0 / DG_B, 0.0).astype(BF16)

    w13_all = ffn_w13.astype(BF16)
    w2_all = ffn_w2.astype(BF16)
    for l in range(depth):
        k = l // 2

        def ffn(x2, s, i, l=l):
            return _ffn_call(
                x2, mods[l, :, s], _row(norm_pre[l, s]), _row(norm_post[l, s]),
                w13_all, w2_all, l, i, res_w=0.5, rows_per_batch=seq)

        x2 = ffn(x2, 0, 0)
        if l % 2 == 0:
            gate_w, gate_b = _block_diag_gate(a_gate_w[k], a_gate_b[k])
            x2 = _ab_call(
                x2, mods[l, :, 1], _row(norm_pre[l, 1]), _row(norm_post[l, 1]),
                _ab_in_columns(ab_w_in[k]).astype(BF16), a_conv_w[k], _row(a_conv_b[k]), gate_w, gate_b,
                _row(a_lam[k]), b_conv_w[k], _row(b_conv_b[k]), _row(b_norm_g[k]),
                _row(b_norm_b[k]), avg, ab_w_out[k].astype(BF16),
                batch=bsz, rows_per_batch=seq)
        else:
            bias_s = jnp.repeat(jnp.transpose(c_b_s[k]), DH_C, axis=1)
            x2 = _c_call(
                x2, mods[l, :, 1], _row(norm_pre[l, 1]), _row(norm_post[l, 1]),
                c_w_in[k].astype(BF16), _row(c_b_in[k]), _row(c_norm_g[k]), _row(c_norm_b[k]),
                c_w_s[k], bias_s, c_w_out[k].astype(BF16), rows_per_batch=seq)
        x2 = ffn(x2, 2, 1)
    return x2.reshape(bsz, seq, d)
```

```python
import functools

import jax
import jax.numpy as jnp
from jax import lax
from jax.experimental import pallas as pl
from jax.experimental.pallas import tpu as pltpu

D_MODEL = 1024
D_FF = 2816
N_SUB = 3
W_A = D_MODEL // 2
W_B = D_MODEL // 2
H_A = 8
DH_A = W_A // H_A
G_B = 8
DG_B = W_B // G_B
CONV_A = 4
CONV_B = 31
LRU_C = 8.0
W_C = D_MODEL
H_C = 8
DH_C = W_C // H_C
CHUNK = 128
EPS = 1e-6

F32 = jnp.float32
BF16 = jnp.bfloat16

VMEM_LIMIT_BYTES = 56 * 1024 * 1024
SUBLANES = 8
FFN_ROWS = 1024
FFN_CHUNK = 256
FFN_ROW_BLOCK = 512
MIX_ROWS = 1024
ADA_COLS = 1536
ADA_ROWS = 16
HALO_A = 8
HALO_B = 32
PROJ_CHUNK = 256
HEAD_ROWS = 512
MIX_TAIL_ROWS = 512
CONV_ROWS = 128
LANES = 128
CONV_LANES = LANES
SEG_PITCH = 136


def _const_spec(shape):
    nd = len(shape)
    return pl.BlockSpec(shape, lambda *_: (0,) * nd, pipeline_mode=pl.Buffered(1))


def _prenorm(x, mod, pre_g):
    ms = jnp.mean(x * x, axis=-1, keepdims=True)
    return (x * lax.rsqrt(ms + EPS)) * (pre_g * (1.0 + mod[1:2])) + mod[0:1]


def _postnorm_residual(x, y, mod, post_g, res_w):
    ms = jnp.mean(y * y, axis=-1, keepdims=True)
    return x + (y * lax.rsqrt(ms + EPS)) * ((res_w * (1.0 + mod[2:3])) * post_g)


def _dot(a, b):
    return jnp.dot(a, b, preferred_element_type=F32)


def _gelu_tanh(x):
    return jax.nn.gelu(x, approximate=True)


def _ada_kernel(c_ref, w_ref, b_ref, o_ref):
    ca = jax.nn.silu(c_ref[...]).astype(BF16)
    o_ref[0] = _dot(ca, w_ref[0].astype(BF16)) + b_ref[0]


def _ada_call(c_pad, ada_w, ada_b):
    depth, d, n_out = ada_w.shape
    return pl.pallas_call(
        _ada_kernel,
        grid=(depth, n_out // ADA_COLS),
        in_specs=[
            pl.BlockSpec((ADA_ROWS, d), lambda l, j: (0, 0)),
            pl.BlockSpec((1, d, ADA_COLS), lambda l, j: (l, 0, j)),
            pl.BlockSpec((1, 1, ADA_COLS), lambda l, j: (l, 0, j)),
        ],
        out_specs=pl.BlockSpec((1, ADA_ROWS, ADA_COLS), lambda l, j: (l, 0, j)),
        out_shape=jax.ShapeDtypeStruct((depth, ADA_ROWS, n_out), F32),
        compiler_params=pltpu.CompilerParams(
            dimension_semantics=("arbitrary", "arbitrary"),
            vmem_limit_bytes=VMEM_LIMIT_BYTES),
        name="ada_proj",
    )(c_pad, ada_w, ada_b.reshape(depth, 1, n_out))


def _ffn_kernel(x_ref, mod_ref, pre_ref, post_ref, w13_ref, w2_ref, o_ref, h_ref, acc_ref, *,
                res_w):
    mod = mod_ref[0]
    pre_g = pre_ref[...]
    post_g = post_ref[...]
    n_chunks = D_FF // FFN_CHUNK
    row_blocks = [slice(r0, r0 + FFN_ROW_BLOCK) for r0 in range(0, x_ref.shape[0], FFN_ROW_BLOCK)]

    def chunk(rows, c):
        lo = c * FFN_CHUNK
        h = h_ref[rows, :]
        g = _dot(h, w13_ref[:, lo:lo + FFN_CHUNK])
        u = _dot(h, w13_ref[:, D_FF + lo:D_FF + lo + FFN_CHUNK])
        act = (jax.nn.silu(g) * u).astype(BF16)
        return _dot(act, w2_ref[lo:lo + FFN_CHUNK, :])

    for rows in row_blocks:
        h_ref[rows, :] = _prenorm(x_ref[rows, :], mod, pre_g).astype(BF16)
        acc_ref[rows, :] = chunk(rows, 0)
    for c in range(1, n_chunks - 1):
        acc_ref[...] += chunk(slice(None), c)
    for rows in row_blocks:
        y = acc_ref[rows, :] + chunk(rows, n_chunks - 1)
        o_ref[rows, :] = _postnorm_residual(x_ref[rows, :], y, mod, post_g, res_w)


def _ffn_call(x2, mod, pre_g, post_g, w13_all, w2_all, layer, which, *, res_w, rows_per_batch):
    n, d = x2.shape
    tiles_per_batch = rows_per_batch // FFN_ROWS
    single = pl.Buffered(1)
    return pl.pallas_call(
        functools.partial(_ffn_kernel, res_w=res_w),
        grid=(n // FFN_ROWS,),
        in_specs=[
            pl.BlockSpec((FFN_ROWS, d), lambda i: (i, 0)),
            pl.BlockSpec((1, 3, d), lambda i: (i // tiles_per_batch, 0, 0)),
            _const_spec((1, d)),
            _const_spec((1, d)),
            pl.BlockSpec((None, None) + w13_all.shape[2:], lambda i: (layer, which, 0, 0),
                         pipeline_mode=single),
            pl.BlockSpec((None, None) + w2_all.shape[2:], lambda i: (layer, which, 0, 0),
                         pipeline_mode=single),
        ],
        out_specs=pl.BlockSpec((FFN_ROWS, d), lambda i: (i, 0)),
        out_shape=jax.ShapeDtypeStruct((n, d), F32),
        scratch_shapes=[pltpu.VMEM((FFN_ROWS, d), BF16), pltpu.VMEM((FFN_ROWS, d), F32)],
        compiler_params=pltpu.CompilerParams(
            dimension_semantics=("arbitrary",),
            vmem_limit_bytes=VMEM_LIMIT_BYTES),
        name="ffn_sublayer",
    )(x2, mod, pre_g, post_g, w13_all, w2_all)


def _causal_conv(buf_ref, w_ref, b_ref, out_ref, *, taps, halo, rows, slabs=None):
    base = halo - (taps - 1)
    if slabs is None:
        slabs = range(buf_ref.shape[-1] // CONV_LANES)
    groups = {}
    for k in range(taps):
        groups.setdefault((base + k) % SUBLANES, []).append(((base + k) // SUBLANES, k))
    n_tiles = CONV_ROWS // SUBLANES
    sub_id = lax.broadcasted_iota(jnp.int32, (n_tiles, SUBLANES, CONV_LANES), 1)
    for l0 in (sl * CONV_LANES for sl in slabs):
        lanes = slice(l0, l0 + CONV_LANES)
        for r0 in range(0, rows, CONV_ROWS):
            y = jnp.broadcast_to(b_ref[:, lanes], (CONV_ROWS, CONV_LANES))
            for s, members in sorted(groups.items()):
                ext = CONV_ROWS + (SUBLANES if s else 0)
                z = None
                for q, k in members:
                    lo = r0 + q * SUBLANES
                    term = w_ref[k:k + 1, lanes] * buf_ref[lo:lo + ext, lanes]
                    z = term if z is None else z + term
                if s:
                    head = z[0:CONV_ROWS].reshape(n_tiles, SUBLANES, CONV_LANES)
                    tail = z[SUBLANES:].reshape(n_tiles, SUBLANES, CONV_LANES)
                    z = pltpu.roll(jnp.where(sub_id >= s, head, tail), SUBLANES - s, axis=1)
                    z = z.reshape(CONV_ROWS, CONV_LANES)
                y = y + z
            out_ref[r0:r0 + CONV_ROWS, lanes] = y


def _sublane_scan(c, p, sub_id):
    for d in (1, 2, 4):
        keep = sub_id >= d
        c_prev = jnp.where(keep, pltpu.roll(c, d, 0), 0.0)
        p_prev = jnp.where(keep, pltpu.roll(p, d, 0), 1.0)
        c = c + p * c_prev
        p = p * p_prev
    return c


def _linear_scan(a, u, a_seg, u_seg, h_seg, p_seg, carry_ref, out_ref):
    rows, width = a.shape
    seg = rows // SUBLANES
    n_slabs = width // LANES
    sub_id = lax.broadcasted_iota(jnp.int32, (SUBLANES, LANES), 0)
    for l in range(n_slabs):
        lanes = slice(l * LANES, (l + 1) * LANES)
        for j in range(SUBLANES):
            a_seg[l, j * SEG_PITCH:j * SEG_PITCH + seg, :] = a[j * seg:(j + 1) * seg, lanes]
            u_seg[l, j * SEG_PITCH:j * SEG_PITCH + seg, :] = u[j * seg:(j + 1) * seg, lanes]
    for l in range(n_slabs):
        lanes = slice(l * LANES, (l + 1) * LANES)
        h = jnp.zeros((SUBLANES, LANES), F32)
        p = jnp.ones((SUBLANES, LANES), F32)
        for k in range(seg):
            step = pl.ds(k, SUBLANES, stride=SEG_PITCH)
            a_k = a_seg[l, step, :]
            h = a_k * h + u_seg[l, step, :]
            p = a_k * p
            h_seg[l, step, :] = h
            p_seg[l, step, :] = p
        first = sub_id == 0
        c_in = _sublane_scan(jnp.where(first, carry_ref[:, lanes], pltpu.roll(h, 1, 0)),
                             jnp.where(first, 0.0, pltpu.roll(p, 1, 0)), sub_id)
        end = h + p * c_in
        carry_ref[:, lanes] = jnp.broadcast_to(end[SUBLANES - 1:SUBLANES, :], (SUBLANES, LANES))
        for j in range(SUBLANES):
            rows_j = slice(j * SEG_PITCH, j * SEG_PITCH + seg)
            out_ref[j * seg:(j + 1) * seg, lanes] = (
                h_seg[l, rows_j, :] + p_seg[l, rows_j, :] * c_in[j:j + 1, :])


def _ab_kernel(x_ref, mod_ref, pre_ref, post_ref, win_ref, acw_ref, acb_ref, gw_ref, gb_ref,
               lam_ref, bcw_ref, bcb_ref, ng_ref, nb_ref, avg_ref, wout_ref, o_ref,
               h_ref, ax_buf, v_buf, t_buf, vc_buf, ya_ref, a_seg, u_seg, h_seg, p_seg, hc_ref):
    rows = x_ref.shape[0]
    j = pl.program_id(1)

    @pl.when(j == 0)
    def _():
        ax_buf[0:HALO_A, :] = jnp.zeros((HALO_A, W_A), F32)
        v_buf[0:HALO_B, :] = jnp.zeros((HALO_B, W_B), F32)
        hc_ref[...] = jnp.zeros_like(hc_ref)

    mod = mod_ref[0]
    pre_g = pre_ref[...]
    for r0 in range(0, rows, HEAD_ROWS):
        rb = slice(r0, r0 + HEAD_ROWS)
        h_ref[rb, :] = _prenorm(x_ref[rb, :], mod, pre_g).astype(BF16)
        ax_buf[HALO_A + r0:HALO_A + r0 + HEAD_ROWS, :] = _dot(h_ref[rb, :], win_ref[:, 0:W_A])

    def in_proj(c0, width):
        return _dot(h_ref[...], win_ref[:, c0:c0 + width])

    def glu_slab(sl):
        zb = in_proj(W_A + 2 * CONV_LANES * sl, 2 * CONV_LANES)
        lanes = slice(sl * CONV_LANES, (sl + 1) * CONV_LANES)
        v_buf[HALO_B:HALO_B + rows, lanes] = (
            zb[:, 0:CONV_LANES] * jax.nn.sigmoid(zb[:, CONV_LANES:]))

    assert W_B // CONV_LANES == 4
    glu_slab(0)
    _causal_conv(ax_buf, acw_ref, acb_ref, t_buf, taps=CONV_A, halo=HALO_A, rows=rows)
    ax_buf[0:HALO_A, :] = ax_buf[rows:rows + HALO_A, :]
    glu_slab(1)
    log_lam = LRU_C * jax.nn.log_sigmoid(lam_ref[...])
    a_blocks, u_blocks = [], []
    for r0 in range(0, rows, HEAD_ROWS):
        xr = t_buf[r0:r0 + HEAD_ROWS, :]
        gates = _dot(xr.astype(BF16), gw_ref[...]) + gb_ref[...]
        r_gate = jax.nn.sigmoid(gates[:, 0:W_A])
        i_gate = jax.nn.sigmoid(gates[:, W_A:])
        log_a = r_gate * log_lam
        th = jnp.tanh(log_a)
        a_blocks.append(jnp.exp(log_a))
        u_blocks.append(jnp.sqrt((-2.0 * th) / (1.0 - th)) * (i_gate * xr))
    a_t = jnp.concatenate(a_blocks, axis=0)
    u_t = jnp.concatenate(u_blocks, axis=0)
    glu_slab(2)
    _linear_scan(a_t, u_t, a_seg, u_seg, h_seg, p_seg, hc_ref, t_buf)
    glu_slab(3)
    y_a = t_buf[...] * _gelu_tanh(in_proj(W_A + 2 * W_B, W_A))
    ya_ref[...] = _dot(y_a.astype(BF16), wout_ref[0:W_A, :])

    for sl in range(W_B // CONV_LANES):
        _causal_conv(v_buf, bcw_ref, bcb_ref, vc_buf, taps=CONV_B, halo=HALO_B, rows=rows,
                     slabs=(sl,))
    v_buf[0:HALO_B, :] = v_buf[rows:rows + HALO_B, :]

    avg = avg_ref[...]
    post_g = post_ref[...]

    def group_mean(t):
        hi = t.astype(BF16)
        lo = (t - hi.astype(F32)).astype(BF16)
        return _dot(hi, avg) + _dot(lo, avg)

    for r0 in range(0, rows, HEAD_ROWS):
        rb = slice(r0, r0 + HEAD_ROWS)
        v = vc_buf[rb, :]
        cen = v - group_mean(v)
        var = group_mean(cen * cen)
        vn = cen * lax.rsqrt(var + EPS) * ng_ref[...] + nb_ref[...]
        y_b = jax.nn.silu(vn).astype(BF16)
        y = ya_ref[rb, :] + _dot(y_b, wout_ref[W_A:, :])
        o_ref[rb, :] = _postnorm_residual(x_ref[rb, :], y, mod, post_g, 1.0)


def _ab_call(x2, mod, pre_g, post_g, w_in, a_conv_w, a_conv_b, gate_w, gate_b, lam,
             b_conv_w, b_conv_b, norm_g, norm_b, avg, w_out, *, batch, rows_per_batch):
    n, d = x2.shape
    rows = MIX_ROWS
    tiles = rows_per_batch // rows
    consts = (pre_g, post_g, w_in, a_conv_w, a_conv_b, gate_w, gate_b, lam,
              b_conv_w, b_conv_b, norm_g, norm_b, avg, w_out)
    return pl.pallas_call(
        _ab_kernel,
        grid=(batch, tiles),
        in_specs=[
            pl.BlockSpec((rows, d), lambda b, j: (b * tiles + j, 0)),
            pl.BlockSpec((1, 3, d), lambda b, j: (b, 0, 0)),
        ] + [_const_spec(a.shape) for a in consts],
        out_specs=pl.BlockSpec((rows, d), lambda b, j: (b * tiles + j, 0)),
        out_shape=jax.ShapeDtypeStruct((n, d), F32),
        scratch_shapes=[
            pltpu.VMEM((rows, d), BF16),
            pltpu.VMEM((HALO_A + rows, W_A), F32),
            pltpu.VMEM((HALO_B + rows, W_B), F32),
            pltpu.VMEM((rows, W_A), F32),
            pltpu.VMEM((rows, W_B), F32),
            pltpu.VMEM((rows, d), F32),
        ] + [pltpu.VMEM((W_A // LANES, SUBLANES * SEG_PITCH, LANES), F32)] * 4 + [
            pltpu.VMEM((SUBLANES, W_A), F32),
        ],
        compiler_params=pltpu.CompilerParams(
            dimension_semantics=("arbitrary", "arbitrary"),
            vmem_limit_bytes=VMEM_LIMIT_BYTES),
        name="mixer_ab_sublayer",
    )(x2, mod, *consts)


def _c_kernel(x_ref, mod_ref, pre_ref, post_ref, win_ref, bin_ref, ng_ref, nb_ref,
              ws_ref, bs_ref, wout_ref, o_ref, h_ref, z_buf):
    rows = x_ref.shape[0]
    mod = mod_ref[0]
    def project(rb, c0):
        cols = slice(c0, c0 + PROJ_CHUNK)
        z_buf[rb, cols] = _gelu_tanh(_dot(h_ref[rb, :], win_ref[:, cols]) + bin_ref[:, cols])

    pre_g = pre_ref[...]
    for r0 in range(0, rows, HEAD_ROWS):
        rb = slice(r0, r0 + HEAD_ROWS)
        h_ref[rb, :] = _prenorm(x_ref[rb, :], mod, pre_g).astype(BF16)
        project(rb, W_C)
    for c0 in list(range(W_C + PROJ_CHUNK, 2 * W_C, PROJ_CHUNK)) + list(range(0, W_C, PROJ_CHUNK)):
        project(slice(None), c0)
    t_id = lax.broadcasted_iota(jnp.int32, (CHUNK, CHUNK), 0)
    s_id = lax.broadcasted_iota(jnp.int32, (CHUNK, CHUNK), 1)
    causal = s_id <= t_id
    ws = [jnp.where(causal, ws_ref[hd], 0.0).astype(BF16) for hd in range(H_C)]
    post_g = post_ref[...]
    blk_chunks = MIX_TAIL_ROWS // CHUNK
    for r0 in range(0, rows, MIX_TAIL_ROWS):
        rb = slice(r0, r0 + MIX_TAIL_ROWS)
        v = z_buf[rb, W_C:]
        mu = jnp.mean(v, axis=-1, keepdims=True)
        cen = v - mu
        var = jnp.mean(cen * cen, axis=-1, keepdims=True)
        vb = (cen * lax.rsqrt(var + EPS) * ng_ref[...] + nb_ref[...]).astype(BF16)
        per_head = []
        for hd in range(H_C):
            c0 = hd * DH_C
            vh = jnp.concatenate(
                [vb[n * CHUNK:(n + 1) * CHUNK, c0:c0 + DH_C] for n in range(blk_chunks)], axis=1)
            per_head.append(_dot(ws[hd], vh))
        mixed = jnp.concatenate(
            [jnp.concatenate([m[:, n * DH_C:(n + 1) * DH_C] for m in per_head], axis=1)
             + bs_ref[...] for n in range(blk_chunks)], axis=0)
        y = _dot((z_buf[rb, 0:W_C] * mixed).astype(BF16), wout_ref[...])
        o_ref[rb, :] = _postnorm_residual(x_ref[rb, :], y, mod, post_g, 1.0)


def _c_call(x2, mod, pre_g, post_g, w_in, b_in, norm_g, norm_b, w_s, bias_s, w_out,
            *, rows_per_batch):
    n, d = x2.shape
    rows = MIX_ROWS
    tiles_per_batch = rows_per_batch // rows
    consts = (pre_g, post_g, w_in, b_in, norm_g, norm_b, w_s, bias_s, w_out)
    return pl.pallas_call(
        _c_kernel,
        grid=(n // rows,),
        in_specs=[
            pl.BlockSpec((rows, d), lambda i: (i, 0)),
            pl.BlockSpec((1, 3, d), lambda i: (i // tiles_per_batch, 0, 0)),
        ] + [_const_spec(a.shape) for a in consts],
        out_specs=pl.BlockSpec((rows, d), lambda i: (i, 0)),
        out_shape=jax.ShapeDtypeStruct((n, d), F32),
        scratch_shapes=[pltpu.VMEM((rows, d), BF16), pltpu.VMEM((rows, 2 * W_C), F32)],
        compiler_params=pltpu.CompilerParams(
            dimension_semantics=("arbitrary",),
            vmem_limit_bytes=VMEM_LIMIT_BYTES),
        name="mixer_c_sublayer",
    )(x2, mod, *consts)


def _row(v):
    return v.reshape(1, -1)


def _ab_in_columns(w_in):
    a_gate, a_x, b_val, b_gate = jnp.split(w_in, [W_A, 2 * W_A, 2 * W_A + W_B], axis=1)
    slabs = []
    for l0 in range(0, W_B, CONV_LANES):
        slabs += [b_val[:, l0:l0 + CONV_LANES], b_gate[:, l0:l0 + CONV_LANES]]
    return jnp.concatenate([a_x] + slabs + [a_gate], axis=1)


def _block_diag_gate(gate_w, gate_b):
    eye = jnp.eye(H_A, dtype=gate_w.dtype)
    w_r = jnp.einsum("hde,hg->hdge", gate_w[:, :, :DH_A], eye).reshape(W_A, W_A)
    w_i = jnp.einsum("hde,hg->hdge", gate_w[:, :, DH_A:], eye).reshape(W_A, W_A)
    w = jnp.concatenate([w_r, w_i], axis=1).astype(BF16)
    b = jnp.concatenate([gate_b[:, :DH_A].reshape(-1), gate_b[:, DH_A:].reshape(-1)])
    return w, _row(b)


def kernel(x, c, ada_w, ada_b, norm_pre, norm_post, ffn_w13, ffn_w2, ab_w_in, a_conv_w, a_conv_b, a_gate_w, a_gate_b, a_lam, b_conv_w, b_conv_b, b_norm_g, b_norm_b, ab_w_out, c_w_in, c_b_in, c_norm_g, c_norm_b, c_w_s, c_b_s, c_w_out):
    bsz, seq, d = x.shape
    depth = ada_w.shape[0]
    assert d == D_MODEL and seq % FFN_ROWS == 0 and seq % MIX_ROWS == 0 and bsz <= ADA_ROWS
    x2 = x.reshape(bsz * seq, d)

    c_pad = jnp.zeros((ADA_ROWS, d), F32).at[:bsz].set(c)
    mods = _ada_call(c_pad, ada_w, ada_b)[:, :bsz].reshape(depth, bsz, N_SUB, 3, d)

    group_id = jnp.arange(W_B) // DG_B
    avg = jnp.where(group_id[:, None] == group_id[None, :], 1.0 / DG_B, 0.0).astype(BF16)

    w13_all = ffn_w13.astype(BF16)
    w2_all = ffn_w2.astype(BF16)
    for l in range(depth):
        k = l // 2

        def ffn(x2, s, i, l=l):
            return _ffn_call(
                x2, mods[l, :, s], _row(norm_pre[l, s]), _row(norm_post[l, s]),
                w13_all, w2_all, l, i, res_w=0.5, rows_per_batch=seq)

        x2 = ffn(x2, 0, 0)
        if l % 2 == 0:
            gate_w, gate_b = _block_diag_gate(a_gate_w[k], a_gate_b[k])
            x2 = _ab_call(
                x2, mods[l, :, 1], _row(norm_pre[l, 1]), _row(norm_post[l, 1]),
                _ab_in_columns(ab_w_in[k]).astype(BF16), a_conv_w[k], _row(a_conv_b[k]), gate_w, gate_b,
                _row(a_lam[k]), b_conv_w[k], _row(b_conv_b[k]), _row(b_norm_g[k]),
                _row(b_norm_b[k]), avg, ab_w_out[k].astype(BF16),
                batch=bsz, rows_per_batch=seq)
        else:
            bias_s = jnp.repeat(jnp.transpose(c_b_s[k]), DH_C, axis=1)
            x2 = _c_call(
                x2, mods[l, :, 1], _row(norm_pre[l, 1]), _row(norm_post[l, 1]),
                c_w_in[k].astype(BF16), _row(c_b_in[k]), _row(c_norm_g[k]), _row(c_norm_b[k]),
                c_w_s[k], bias_s, c_w_out[k].astype(BF16), rows_per_batch=seq)
        x2 = ffn(x2, 2, 1)
    return x2.reshape(bsz, seq, d)
```
